```python
import math
import jax, jax.numpy as jnp
from jax import lax
import numpy as np

D_MODEL = 1024
BATCH = 4
SEQ = 4096
DEPTH = 4
DEC_BATCH = 128
DEC_SEQ = 1
PAST_LEN = 8192
PAGE_SIZE = 128

N_MIXERS = 3
H_A = 8
KV_A = 4
G_A = H_A // KV_A
HD_A = 64
Q_BLOCK = 128
H_B = 16
KV_B = 2
G_B = H_B // KV_B
HD_B = 64
WINDOW = 128
H_C = 4
DK_C = 128
DV_C = 256
GATE_RANK = 16
GATE_TAU = 16.0
GLA_CHUNK = 32
D_FF = 2816
N_EXPERTS = 8
TOP_K = 2
D_FF_E = 2816
LN_EPS = 1e-5
RMS_EPS = 1e-6
ALPHA = (2 * DEPTH) ** 0.25
BETA = (8 * DEPTH) ** -0.25
N_A = (DEPTH + 2) // 3
N_B = (DEPTH + 1) // 3
N_C = DEPTH // 3
N_DENSE = (DEPTH + 1) // 2
N_MOE = DEPTH // 2
F32 = jnp.float32

kernel_name = 'hybrid_diffattn_swa_gla_decoder_step'


def layer_norm(x, g, b):
    xf = x.astype(F32)
    mu = xf.mean(-1, keepdims=True)
    var = jnp.square(xf - mu).mean(-1, keepdims=True)
    return ((xf - mu) * lax.rsqrt(var + LN_EPS) * g.astype(F32) + b.astype(F32)).astype(x.dtype)


def rms_norm(x, g):
    xf = x.astype(F32)
    return (xf * lax.rsqrt(jnp.square(xf).mean(-1, keepdims=True) + RMS_EPS) * g.astype(F32)).astype(x.dtype)


def alibi_slopes(n):
    return jnp.exp2(-8.0 * jnp.arange(1, n + 1, dtype=F32) / n)


def diff_split(x, w_qkv):
    b, t, _ = x.shape
    h = x @ w_qkv
    nq = H_A * 2 * HD_A
    nk = KV_A * 2 * HD_A
    q = h[..., :nq].reshape(b, t, KV_A, G_A, 2, HD_A)
    k = h[..., nq:nq + nk].reshape(b, t, KV_A, 2 * HD_A)
    v = h[..., nq + nk:].reshape(b, t, KV_A, 2 * HD_A)
    return q, k, v


def diff_scores(q, k):
    kk = k.reshape(k.shape[0], k.shape[1], KV_A, 2, HD_A)
    return jnp.einsum('bqkgmd,bskmd->bkgmqs', q, kk).astype(F32) * (HD_A ** -0.5)


def diff_weights(s, dist, lam_full):
    slopes = alibi_slopes(H_A).reshape(KV_A, G_A)
    dist = dist.astype(F32)
    s = jnp.where(dist >= 0, s - slopes[:, :, None, None, None] * dist, -jnp.inf)
    p = jax.nn.softmax(s, axis=-1)
    return p[:, :, :, 0] - lam_full * p[:, :, :, 1]


def diff_lambda(lam, layer_idx):
    lam_init = 0.8 - 0.6 * math.exp(-0.3 * layer_idx)
    lf = lam.astype(F32)
    lam_full = jnp.exp(jnp.sum(lf[0] * lf[1])) - jnp.exp(jnp.sum(lf[2] * lf[3])) + lam_init
    return lam_full, lam_init


def diff_out(o, subln, lam_init, w_o):
    b, t = o.shape[:2]
    o = rms_norm(o, subln) * (1.0 - lam_init)
    return o.reshape(b, t, H_A * 2 * HD_A) @ w_o


def diff_attn_prompt(x, w_qkv, lam, subln, w_o, layer_idx):
    b, t, _ = x.shape
    q, k, v = diff_split(x, w_qkv)
    lam_full, lam_init = diff_lambda(lam, layer_idx)
    nb = t // Q_BLOCK
    qb = q.reshape(b, nb, Q_BLOCK, KV_A, G_A, 2, HD_A).swapaxes(0, 1)
    spos = jnp.arange(t)

    def block(args):
        q_blk, bi = args
        tpos = bi * Q_BLOCK + jnp.arange(Q_BLOCK)
        a = diff_weights(diff_scores(q_blk, k), tpos[:, None] - spos[None, :], lam_full)
        return jnp.einsum('bkgqs,bskd->bqkgd', a.astype(v.dtype), v)

    o = lax.map(block, (qb, jnp.arange(nb)))
    o = o.swapaxes(0, 1).reshape(b, t, KV_A, G_A, 2 * HD_A)
    return diff_out(o, subln, lam_init, w_o), k, v


def diff_attn_sample(x, k_past, v_past, w_qkv, lam, subln, w_o, layer_idx):
    b, t, _ = x.shape
    past = k_past.shape[1]
    q, k, v = diff_split(x, w_qkv)
    lam_full, lam_init = diff_lambda(lam, layer_idx)
    tpos = past + jnp.arange(t)
    spos = jnp.concatenate([jnp.arange(past), tpos])
    s = jnp.concatenate([diff_scores(q, k_past), diff_scores(q, k)], axis=-1)
    a = diff_weights(s, tpos[:, None] - spos[None, :], lam_full).astype(v.dtype)
    o = (jnp.einsum('bkgqs,bskd->bqkgd', a[..., :past], v_past)
         + jnp.einsum('bkgqs,bskd->bqkgd', a[..., past:], v))
    return diff_out(o, subln, lam_init, w_o), k, v


def swa_split(x, w_qkv, b_qkv):
    b, t, _ = x.shape
    h = x @ w_qkv + b_qkv
    nq = H_B * HD_B
    nk = KV_B * HD_B
    q = h[..., :nq].reshape(b, t, KV_B, G_B, HD_B)
    k = h[..., nq:nq + nk].reshape(b, t, KV_B, HD_B)
    v = h[..., nq + nk:].reshape(b, t, KV_B, HD_B)
    return q, k, v


def sink_probs(s, dist, valid, sinks):
    slopes = alibi_slopes(H_B).reshape(KV_B, G_B)
    s = jnp.where(valid, s - slopes[:, :, None, None] * dist.astype(F32), -jnp.inf)
    sk = jnp.broadcast_to(sinks.astype(F32).reshape(KV_B, G_B)[:, :, None, None], s.shape[:-1] + (1,))
    return jax.nn.softmax(jnp.concatenate([s, sk], axis=-1), axis=-1)[..., :-1]


def swa_attn_prompt(x, w_qkv, b_qkv, sinks, w_o, b_o):
    b, t, _ = x.shape
    q, k, v = swa_split(x, w_qkv, b_qkv)
    nb = t // WINDOW
    qb = q.reshape(b, nb, WINDOW, KV_B, G_B, HD_B)
    kb = k.reshape(b, nb, WINDOW, KV_B, HD_B)
    vb = v.reshape(b, nb, WINDOW, KV_B, HD_B)
    pad = ((0, 0), (1, 0), (0, 0), (0, 0), (0, 0))
    k_band = jnp.concatenate([jnp.pad(kb, pad)[:, :-1], kb], axis=2)
    v_band = jnp.concatenate([jnp.pad(vb, pad)[:, :-1], vb], axis=2)
    rel = (WINDOW + jnp.arange(WINDOW))[:, None] - jnp.arange(2 * WINDOW)[None, :]
    has_prev = (jnp.arange(nb) > 0)[:, None, None] | (jnp.arange(2 * WINDOW) >= WINDOW)[None, None, :]
    valid = ((rel >= 0) & (rel <= WINDOW))[None] & has_prev
    s = jnp.einsum('bnqkgd,bnskd->bnkgqs', qb, k_band).astype(F32) * (HD_B ** -0.5)
    p = sink_probs(s, rel, valid[:, None, None], sinks)
    o = jnp.einsum('bnkgqs,bnskd->bnqkgd', p.astype(v.dtype), v_band)
    y = o.reshape(b, t, H_B * HD_B) @ w_o + b_o
    return y, k[:, t - WINDOW:], v[:, t - WINDOW:]


def swa_attn_sample(x, k_buf, v_buf, w_qkv, b_qkv, sinks, w_o, b_o):
    b, t, _ = x.shape
    q, k, v = swa_split(x, w_qkv, b_qkv)
    keys = jnp.concatenate([k_buf, k], axis=1)
    vals = jnp.concatenate([v_buf, v], axis=1)
    dist = (WINDOW + jnp.arange(t))[:, None] - jnp.arange(WINDOW + t)[None, :]
    valid = (dist >= 0) & (dist <= WINDOW)
    s = jnp.einsum('bqkgd,bskd->bkgqs', q, keys).astype(F32) * (HD_B ** -0.5)
    p = sink_probs(s, dist, valid, sinks)
    o = jnp.einsum('bkgqs,bskd->bqkgd', p.astype(vals.dtype), vals)
    y = o.reshape(b, t, H_B * HD_B) @ w_o + b_o
    return y, keys[:, -WINDOW:], vals[:, -WINDOW:]


def gla_split(x, w_in, w_gate2, b_gate, b_r):
    b, t, _ = x.shape
    h = x @ w_in
    dk = H_C * DK_C
    dv = H_C * DV_C
    q = h[..., :dk].reshape(b, t, H_C, DK_C)
    k = h[..., dk:2 * dk].reshape(b, t, H_C, DK_C)
    v = h[..., 2 * dk:2 * dk + dv].reshape(b, t, H_C, DV_C)
    r = jax.nn.silu(h[..., 2 * dk + dv:2 * dk + 2 * dv] + b_r)
    z = h[..., 2 * dk + 2 * dv:] @ w_gate2 + b_gate
    g = (jax.nn.log_sigmoid(z.astype(F32)) / GATE_TAU).reshape(b, t, H_C, DK_C)
    return q, k, v, r, g


def gla_chunk(state, q, k, v, g):
    c = q.shape[1]
    q = q.astype(F32) * (DK_C ** -0.5)
    k = k.astype(F32)
    v = v.astype(F32)
    bc = jnp.cumsum(g, axis=1)
    q_dec = q * jnp.exp(bc)
    k_dec = k * jnp.exp(-bc)
    att = jnp.einsum('bchk,bshk->bhcs', q_dec, k_dec)
    att = jnp.where(jnp.tril(jnp.ones((c, c), bool)), att, 0.0)
    o = jnp.einsum('bchk,bhkv->bchv', q_dec, state) + jnp.einsum('bhcs,bshv->bchv', att, v)
    b_last = bc[:, -1]
    k_tail = k * jnp.exp(b_last[:, None] - bc)
    new_state = jnp.exp(b_last)[..., None] * state + jnp.einsum('bshk,bshv->bhkv', k_tail, v)
    return new_state, o


def gla_out(o, r, gn, w_o):
    b, t = o.shape[:2]
    o = rms_norm(o.astype(r.dtype), gn).reshape(b, t, H_C * DV_C) * r
    return o @ w_o


def gla_prompt(x, w_in, w_gate2, b_gate, b_r, gn, w_o):
    b, t, _ = x.shape
    q, k, v, r, g = gla_split(x, w_in, w_gate2, b_gate, b_r)
    nc = t // GLA_CHUNK

    def chunks(a):
        return a.reshape(b, nc, GLA_CHUNK, *a.shape[2:]).swapaxes(0, 1)

    s0 = jnp.zeros((b, H_C, DK_C, DV_C), F32)
    s_fin, o = lax.scan(lambda s, cin: gla_chunk(s, *cin), s0, (chunks(q), chunks(k), chunks(v), chunks(g)))
    o = o.swapaxes(0, 1).reshape(b, t, H_C, DV_C)
    return gla_out(o, r, gn, w_o), s_fin.astype(x.dtype)


def gla_sample(x, state, w_in, w_gate2, b_gate, b_r, gn, w_o):
    q, k, v, r, g = gla_split(x, w_in, w_gate2, b_gate, b_r)
    s_new, o = gla_chunk(state.astype(F32), q, k, v, g)
    return gla_out(o, r, gn, w_o), s_new.astype(x.dtype)


def swiglu(x, w_gu, w_down):
    g, u = jnp.split(x @ w_gu, 2, axis=-1)
    return (jax.nn.silu(g) * u) @ w_down


def moe_swiglu(x, w_router, w_gu, w_down):
    shp = x.shape
    xt = x.reshape(-1, shp[-1])
    logits = (xt @ w_router).astype(F32)
    top_v, top_i = lax.top_k(logits, TOP_K)
    gates = jax.nn.softmax(top_v, axis=-1)
    combine = jnp.einsum('nk,nke->ne', gates, jax.nn.one_hot(top_i, N_EXPERTS, dtype=F32)).astype(x.dtype)
    y = jnp.zeros_like(xt)
    for e in range(N_EXPERTS):
        y = y + combine[:, e:e + 1] * swiglu(xt, w_gu[e], w_down[e])
    return y.reshape(shp)


def setup_inputs(seed: int = 0) -> dict:
    key = jax.random.key(seed)
    keys = iter(jax.random.split(key, 40))

    def nrm(shape, scale):
        return jax.random.normal(next(keys), shape, F32) * scale

    n_pages = PAST_LEN // PAGE_SIZE
    n_used = DEC_BATCH * n_pages
    n_pool = n_used + n_used // 4
    page_table = jax.random.permutation(next(keys), n_pool)[:n_used].reshape(DEC_BATCH, n_pages).astype(jnp.int32)
    d_qkv_a = H_A * 2 * HD_A + 2 * KV_A * 2 * HD_A
    d_qkv_b = (H_B + 2 * KV_B) * HD_B
    d_in_c = 2 * H_C * DK_C + 2 * H_C * DV_C + GATE_RANK
    return {
        'x_prompt': nrm((BATCH, SEQ, D_MODEL), 1.0),
        'x_sample': nrm((DEC_BATCH, DEC_SEQ, D_MODEL), 1.0),
        'cache_k_a': nrm((N_A, n_pool, PAGE_SIZE, KV_A, 2 * HD_A), 1.0),
        'cache_v_a': nrm((N_A, n_pool, PAGE_SIZE, KV_A, 2 * HD_A), 1.0),
        'state_swa_k': nrm((N_B, DEC_BATCH, WINDOW, KV_B, HD_B), 1.0),
        'state_swa_v': nrm((N_B, DEC_BATCH, WINDOW, KV_B, HD_B), 1.0),
        'state_gla': nrm((N_C, DEC_BATCH, H_C, DK_C, DV_C), 1.0),
        'page_table': page_table,
        'w_qkv_a': nrm((N_A, D_MODEL, d_qkv_a), D_MODEL ** -0.5),
        'lam_a': nrm((N_A, 4, HD_A), 0.1),
        'subln_a': 1.0 + nrm((N_A, 2 * HD_A), 0.01),
        'w_o_a': nrm((N_A, H_A * 2 * HD_A, D_MODEL), BETA * (H_A * 2 * HD_A) ** -0.5),
        'w_qkv_b': nrm((N_B, D_MODEL, d_qkv_b), D_MODEL ** -0.5),
        'b_qkv_b': nrm((N_B, d_qkv_b), 0.01),
        'sinks_b': nrm((N_B, H_B), 0.5),
        'w_o_b': nrm((N_B, H_B * HD_B, D_MODEL), BETA * (H_B * HD_B) ** -0.5),
        'b_o_b': nrm((N_B, D_MODEL), 0.01),
        'w_in_c': nrm((N_C, D_MODEL, d_in_c), D_MODEL ** -0.5),
        'w_gate2_c': nrm((N_C, GATE_RANK, H_C * DK_C), GATE_RANK ** -0.5),
        'b_gate_c': nrm((N_C, H_C * DK_C), 0.01),
        'b_r_c': nrm((N_C, H_C * DV_C), 0.01),
        'gn_c': 1.0 + nrm((N_C, DV_C), 0.01),
        'w_o_c': nrm((N_C, H_C * DV_C, D_MODEL), BETA * (H_C * DV_C) ** -0.5),
        'ln1_g': 1.0 + nrm((DEPTH, D_MODEL), 0.01),
        'ln1_b': nrm((DEPTH, D_MODEL), 0.01),
        'ln2_g': 1.0 + nrm((DEPTH, D_MODEL), 0.01),
        'ln2_b': nrm((DEPTH, D_MODEL), 0.01),
        'w_gu_d': nrm((N_DENSE, D_MODEL, 2 * D_FF), D_MODEL ** -0.5),
        'w_down_d': nrm((N_DENSE, D_FF, D_MODEL), BETA * D_FF ** -0.5),
        'w_router': nrm((N_MOE, D_MODEL, N_EXPERTS), D_MODEL ** -0.5),
        'w_gu_e': nrm((N_MOE, N_EXPERTS, D_MODEL, 2 * D_FF_E), D_MODEL ** -0.5),
        'w_down_e': nrm((N_MOE, N_EXPERTS, D_FF_E, D_MODEL), BETA * D_FF_E ** -0.5),
    }


def reference(x_prompt, x_sample, cache_k_a, cache_v_a, state_swa_k, state_swa_v, state_gla, page_table,
              w_qkv_a, lam_a, subln_a, w_o_a, w_qkv_b, b_qkv_b, sinks_b, w_o_b, b_o_b,
              w_in_c, w_gate2_c, b_gate_c, b_r_c, gn_c, w_o_c, ln1_g, ln1_b, ln2_g, ln2_b,
              w_gu_d, w_down_d, w_router, w_gu_e, w_down_e):
    yp, ys = x_prompt, x_sample
    bd = page_table.shape[0]
    ka_p, va_p, ka_s, va_s = [], [], [], []
    kb_p, vb_p, kb_s, vb_s = [], [], [], []
    gc_p, gc_s = [], []
    for i in range(DEPTH):
        j = i // N_MIXERS
        if i % N_MIXERS == 0:
            mp, k_new, v_new = diff_attn_prompt(yp, w_qkv_a[j], lam_a[j], subln_a[j], w_o_a[j], i)
            ka_p.append(k_new)
            va_p.append(v_new)
            k_past = cache_k_a[j, page_table].reshape(bd, -1, KV_A, 2 * HD_A)
            v_past = cache_v_a[j, page_table].reshape(bd, -1, KV_A, 2 * HD_A)
            ms, k_new, v_new = diff_attn_sample(ys, k_past, v_past, w_qkv_a[j], lam_a[j], subln_a[j], w_o_a[j], i)
            ka_s.append(k_new)
            va_s.append(v_new)
        elif i % N_MIXERS == 1:
            mp, k_new, v_new = swa_attn_prompt(yp, w_qkv_b[j], b_qkv_b[j], sinks_b[j], w_o_b[j], b_o_b[j])
            kb_p.append(k_new)
            vb_p.append(v_new)
            ms, k_new, v_new = swa_attn_sample(ys, state_swa_k[j], state_swa_v[j], w_qkv_b[j], b_qkv_b[j],
                                               sinks_b[j], w_o_b[j], b_o_b[j])
            kb_s.append(k_new)
            vb_s.append(v_new)
        else:
            mp, s_new = gla_prompt(yp, w_in_c[j], w_gate2_c[j], b_gate_c[j], b_r_c[j], gn_c[j], w_o_c[j])
            gc_p.append(s_new)
            ms, s_new = gla_sample(ys, state_gla[j], w_in_c[j], w_gate2_c[j], b_gate_c[j], b_r_c[j], gn_c[j], w_o_c[j])
            gc_s.append(s_new)
        yp = layer_norm(ALPHA * yp + mp, ln1_g[i], ln1_b[i])
        ys = layer_norm(ALPHA * ys + ms, ln1_g[i], ln1_b[i])
        f = i // 2
        if i % 2 == 0:
            fp = swiglu(yp, w_gu_d[f], w_down_d[f])
            fs = swiglu(ys, w_gu_d[f], w_down_d[f])
        else:
            fp = moe_swiglu(yp, w_router[f], w_gu_e[f], w_down_e[f])
            fs = moe_swiglu(ys, w_router[f], w_gu_e[f], w_down_e[f])
        yp = layer_norm(ALPHA * yp + fp, ln2_g[i], ln2_b[i])
        ys = layer_norm(ALPHA * ys + fs, ln2_g[i], ln2_b[i])
    return (yp, ys, jnp.stack(ka_p), jnp.stack(va_p), jnp.stack(ka_s), jnp.stack(va_s),
            jnp.stack(kb_p), jnp.stack(vb_p), jnp.stack(kb_s), jnp.stack(vb_s),
            jnp.stack(gc_p), jnp.stack(gc_s))
```

```python
import functools
import math

import jax
import jax.numpy as jnp
import numpy as np
from jax import lax
from jax.experimental import pallas as pl
from jax.experimental.pallas import tpu as pltpu

F32 = jnp.float32
BF16 = jnp.bfloat16

D_MODEL = 1024
DEPTH = 4
PAGE_SIZE = 128
H_A, KV_A, G_A, HD_A = 8, 4, 2, 64
H_B, KV_B, G_B, HD_B = 16, 2, 8, 64
WINDOW = 128
H_C, DK_C, DV_C = 4, 128, 256
GATE_RANK = 16
GATE_TAU = 16.0
D_FF = 2816
N_EXPERTS = 8
LN_EPS = 1e-5
RMS_EPS = 1e-6
ALPHA = (2 * DEPTH) ** 0.25

LANE = 128
VMEM_LIMIT = 56 * 1024 * 1024
NEG_INF = float("-inf")

SAMPLE_PAD = 512
TOK_TILE = 768
FF_TILE = 256
MOE_TILE = 768
Q_TILE = 128
K_TILE = 512
GLA_SUB = 64
GLA_TILE = 256
PAGES_PER_STEP = 8


def _cparams(sem):
    return pltpu.CompilerParams(dimension_semantics=sem, vmem_limit_bytes=VMEM_LIMIT)


def _layer_norm_rows(x, g, b):
    mu = jnp.mean(x, axis=-1, keepdims=True)
    xc = x - mu
    var = jnp.mean(xc * xc, axis=-1, keepdims=True)
    return xc * lax.rsqrt(var + LN_EPS) * g + b


def _proj_kernel(x_ref, w_ref, b_ref, *rest, emit_tok):
    if emit_tok:
        hm_ref, tok_ref, wb_ref = rest
    else:
        hm_ref, wb_ref = rest

    @pl.when(pl.program_id(1) == 0)
    def _():
        wb_ref[...] = w_ref[...].astype(BF16)

    y = jnp.dot(x_ref[...], wb_ref[...], preferred_element_type=F32) + b_ref[...]
    for c in range(hm_ref.shape[0]):
        hm_ref[c] = y[:, c * LANE:(c + 1) * LANE].astype(hm_ref.dtype)
    if emit_tok:
        tok_ref[...] = y


def _proj(xb, w, b, *, layer, col0, ncols, cw, hm_dtype, emit_tok, tm=TOK_TILE):
    nt = xb.shape[0]
    assert ncols % cw == 0 and col0 % cw == 0 and cw % LANE == 0 and nt % tm == 0
    nb = cw // LANE
    j0 = col0 // cw
    out_shape = [jax.ShapeDtypeStruct((ncols // LANE, nt, LANE), hm_dtype)]
    out_specs = [pl.BlockSpec((nb, tm, LANE), lambda j, i: (j, i, 0))]
    if emit_tok:
        out_shape.append(jax.ShapeDtypeStruct((nt, ncols), F32))
        out_specs.append(pl.BlockSpec((tm, cw), lambda j, i: (i, j)))
    return pl.pallas_call(
        functools.partial(_proj_kernel, emit_tok=emit_tok),
        grid=(ncols // cw, nt // tm),
        in_specs=[pl.BlockSpec((tm, D_MODEL), lambda j, i: (i, 0)),
                  pl.BlockSpec((None, D_MODEL, cw), lambda j, i: (layer, 0, j0 + j)),
                  pl.BlockSpec((1, cw), lambda j, i: (0, j))],
        out_specs=out_specs,
        out_shape=out_shape,
        scratch_shapes=[pltpu.VMEM((D_MODEL, cw), BF16)],
        compiler_params=_cparams(("arbitrary", "arbitrary")),
        name="proj",
    )(xb, w, b.reshape(1, ncols))


def _top2_route(logits):
    lane = lax.broadcasted_iota(jnp.int32, logits.shape, 1)
    l1 = jnp.where(lane < N_EXPERTS, logits, NEG_INF)
    m1 = jnp.max(l1, axis=-1, keepdims=True)
    i1 = jnp.min(jnp.where(l1 == m1, lane, LANE), axis=-1, keepdims=True)
    l2 = jnp.where(lane == i1, NEG_INF, l1)
    m2 = jnp.max(l2, axis=-1, keepdims=True)
    i2 = jnp.min(jnp.where(l2 == m2, lane, LANE), axis=-1, keepdims=True)
    e = jnp.exp(m2 - m1)
    g1 = 1.0 / (1.0 + e)
    g2 = e / (1.0 + e)
    return jnp.where(lane == 0, g1,
                     jnp.where(lane == 1, g2,
                               jnp.where(lane == 2, i1.astype(F32),
                                         jnp.where(lane == 3, i2.astype(F32), 0.0))))


def _outproj_kernel(a_ref, w_ref, b_ref, res_ref, g_ref, be_ref, *rest, route):
    if route:
        wr_ref, y_ref, yb_ref, rt_ref, wb_ref = rest
    else:
        y_ref, yb_ref, wb_ref = rest

    @pl.when(pl.program_id(0) == 0)
    def _():
        wb_ref[...] = w_ref[...].astype(BF16)

    m = jnp.dot(a_ref[...], wb_ref[...], preferred_element_type=F32) + b_ref[...]
    y = _layer_norm_rows(ALPHA * res_ref[...] + m, g_ref[...], be_ref[...])
    y_ref[...] = y
    yb_ref[...] = y.astype(BF16)
    if route:
        logits = jnp.dot(y, wr_ref[...], preferred_element_type=F32, precision=lax.Precision.HIGHEST)
        rt_ref[...] = _top2_route(logits)


def _outproj_ln(a, w, layer, b, res, g, be, w_router=None, *, tm=TOK_TILE):
    nt, k = a.shape
    route = w_router is not None
    row = lambda i: (i, 0)
    const = lambda i: (0, 0)
    in_specs = [pl.BlockSpec((tm, k), row), pl.BlockSpec((None, k, D_MODEL), lambda i: (layer, 0, 0)),
                pl.BlockSpec((1, D_MODEL), const), pl.BlockSpec((tm, D_MODEL), row),
                pl.BlockSpec((1, D_MODEL), const), pl.BlockSpec((1, D_MODEL), const)]
    args = [a, w, b.reshape(1, D_MODEL), res, g.reshape(1, D_MODEL), be.reshape(1, D_MODEL)]
    out_shape = [jax.ShapeDtypeStruct((nt, D_MODEL), F32), jax.ShapeDtypeStruct((nt, D_MODEL), BF16)]
    out_specs = [pl.BlockSpec((tm, D_MODEL), row), pl.BlockSpec((tm, D_MODEL), row)]
    if route:
        wr = jnp.zeros((D_MODEL, LANE), F32).at[:, :N_EXPERTS].set(w_router)
        in_specs.append(pl.BlockSpec((D_MODEL, LANE), const))
        args.append(wr)
        out_shape.append(jax.ShapeDtypeStruct((nt, LANE), F32))
        out_specs.append(pl.BlockSpec((tm, LANE), row))
    return pl.pallas_call(
        functools.partial(_outproj_kernel, route=route),
        grid=(nt // tm,),
        in_specs=in_specs, out_specs=out_specs, out_shape=out_shape,
        scratch_shapes=[pltpu.VMEM((k, D_MODEL), BF16)],
        compiler_params=_cparams(("arbitrary",)),
        name="outproj_ln",
    )(*args)


def _ffn_kernel(te_ref, tv_ref, x_ref, wg_ref, wu_ref, wd_ref, *rest, dense):
    if dense:
        res_ref, g_ref, be_ref, y_ref, yb_ref, acc_ref = rest
    else:
        y_ref, acc_ref = rest
    i = pl.program_id(0)
    j = pl.program_id(1)
    nj = pl.num_programs(1)

    @pl.when(tv_ref[i] > 0)
    def _():
        x = x_ref[...]
        gate = jnp.dot(x, wg_ref[...].astype(BF16), preferred_element_type=F32)
        up = jnp.dot(x, wu_ref[...].astype(BF16), preferred_element_type=F32)
        h = (gate * (1.0 / (1.0 + jnp.exp(-gate))) * up).astype(BF16)
        part = jnp.dot(h, wd_ref[...].astype(BF16), preferred_element_type=F32)

        @pl.when(j == 0)
        def _():
            acc_ref[...] = part

        @pl.when(j > 0)
        def _():
            acc_ref[...] += part

        @pl.when(j == nj - 1)
        def _():
            if dense:
                y = _layer_norm_rows(ALPHA * res_ref[...] + acc_ref[...], g_ref[...], be_ref[...])
                y_ref[...] = y
                yb_ref[...] = y.astype(BF16)
            else:
                y_ref[...] = acc_ref[...]

    if not dense:
        @pl.when(jnp.logical_and(tv_ref[i] == 0, j == nj - 1))
        def _():
            y_ref[...] = jnp.zeros_like(y_ref)


def _ffn(xb, w_gu, w_down, tile_expert, tile_valid, *, tm, dense_args=None, tf=FF_TILE):
    m = xb.shape[0]
    assert m % tm == 0 and D_FF % tf == 0
    nf = D_FF // tf
    dense = dense_args is not None

    def jeff(i, j, tv):
        return jnp.where(tv[i] > 0, j, nf - 1)

    in_specs = [
        pl.BlockSpec((tm, D_MODEL), lambda i, j, te, tv: (i, 0)),
        pl.BlockSpec((None, D_MODEL, tf), lambda i, j, te, tv: (te[i], 0, jeff(i, j, tv))),
        pl.BlockSpec((None, D_MODEL, tf), lambda i, j, te, tv: (te[i], 0, nf + jeff(i, j, tv))),
        pl.BlockSpec((None, tf, D_MODEL), lambda i, j, te, tv: (te[i], jeff(i, j, tv), 0)),
    ]
    args = [xb, w_gu, w_gu, w_down]
    row = lambda i, j, te, tv: (i, 0)
    const = lambda i, j, te, tv: (0, 0)
    if dense:
        res, g, be = dense_args
        in_specs += [pl.BlockSpec((tm, D_MODEL), row), pl.BlockSpec((1, D_MODEL), const),
                     pl.BlockSpec((1, D_MODEL), const)]
        args += [res, g.reshape(1, D_MODEL), be.reshape(1, D_MODEL)]
        out_shape = [jax.ShapeDtypeStruct((m, D_MODEL), F32), jax.ShapeDtypeStruct((m, D_MODEL), BF16)]
        out_specs = [pl.BlockSpec((tm, D_MODEL), row), pl.BlockSpec((tm, D_MODEL), row)]
    else:
        out_shape = jax.ShapeDtypeStruct((m, D_MODEL), F32)
        out_specs = pl.BlockSpec((tm, D_MODEL), row)
    return pl.pallas_call(
        functools.partial(_ffn_kernel, dense=dense),
        grid_spec=pltpu.PrefetchScalarGridSpec(
            num_scalar_prefetch=2, grid=(m // tm, nf),
            in_specs=in_specs, out_specs=out_specs,
            scratch_shapes=[pltpu.VMEM((tm, D_MODEL), F32)]),
        out_shape=out_shape,
        compiler_params=_cparams(("arbitrary", "arbitrary")),
        name="ffn_dense" if dense else "ffn_moe",
    )(tile_expert, tile_valid, *args)


def _combine_kernel(ya_ref, yb_ref, rt_ref, res_ref, g_ref, be_ref, y_ref, ybf_ref):
    rt = rt_ref[...]
    f = rt[:, 0:1] * ya_ref[...] + rt[:, 1:2] * yb_ref[...]
    y = _layer_norm_rows(ALPHA * res_ref[...] + f, g_ref[...], be_ref[...])
    y_ref[...] = y
    ybf_ref[...] = y.astype(BF16)


def _combine_ln(ya, yb, route, res, g, be, *, tm=TOK_TILE):
    nt = res.shape[0]
    row = lambda i: (i, 0)
    const = lambda i: (0, 0)
    return pl.pallas_call(
        _combine_kernel,
        grid=(nt // tm,),
        in_specs=[pl.BlockSpec((tm, D_MODEL), row), pl.BlockSpec((tm, D_MODEL), row),
                  pl.BlockSpec((tm, LANE), row), pl.BlockSpec((tm, D_MODEL), row),
                  pl.BlockSpec((1, D_MODEL), const), pl.BlockSpec((1, D_MODEL), const)],
        out_specs=[pl.BlockSpec((tm, D_MODEL), row), pl.BlockSpec((tm, D_MODEL), row)],
        out_shape=[jax.ShapeDtypeStruct((nt, D_MODEL), F32), jax.ShapeDtypeStruct((nt, D_MODEL), BF16)],
        compiler_params=_cparams(("arbitrary",)),
        name="moe_combine_ln",
    )(ya, yb, route, res, g.reshape(1, D_MODEL), be.reshape(1, D_MODEL))


def _moe(yb_in, res, route, n_real, w_gu, w_down, e_off, g, be, *, tm=MOE_TILE):
    nt = yb_in.shape[0]
    gates_idx = route[:n_real, 2:4].astype(jnp.int32)
    flat_e = gates_idx.reshape(-1)
    npair = flat_e.shape[0]
    onehot = (flat_e[:, None] == jnp.arange(N_EXPERTS)[None, :]).astype(jnp.int32)
    csum = jnp.cumsum(onehot, axis=0)
    counts = csum[-1]
    rank = jnp.take_along_axis(csum, flat_e[:, None], axis=1)[:, 0] - 1
    tiles_per = (counts + tm - 1) // tm
    tile_end = jnp.cumsum(tiles_per)
    tile_start = tile_end - tiles_per
    n_tiles = npair // tm + N_EXPERTS
    m_pad = n_tiles * tm
    dest = tile_start[flat_e] * tm + rank
    row_src = jnp.zeros((m_pad,), jnp.int32).at[dest].set(jnp.arange(npair, dtype=jnp.int32) // 2)
    t_ids = jnp.arange(n_tiles, dtype=jnp.int32)
    tile_valid = (t_ids < tile_end[-1]).astype(jnp.int32)
    tile_expert = jnp.minimum(jnp.searchsorted(tile_end, t_ids, side="right"), N_EXPERTS - 1).astype(jnp.int32)
    last_e = tile_expert[jnp.maximum(tile_end[-1] - 1, 0)]
    tile_expert = jnp.where(tile_valid > 0, tile_expert, last_e) + e_off
    x_sorted = jnp.take(yb_in, row_src, axis=0)
    y_sorted = _ffn(x_sorted, w_gu, w_down, tile_expert, tile_valid, tm=tm)
    dest2 = jnp.zeros((nt, 2), jnp.int32).at[:n_real].set(dest.reshape(n_real, 2))
    ya = jnp.take(y_sorted, dest2[:, 0], axis=0)
    ybb = jnp.take(y_sorted, dest2[:, 1], axis=0)
    return _combine_ln(ya, ybb, route, res, g, be)


def _diff_lambda(lam, layer_idx):
    lam_init = 0.8 - 0.6 * math.exp(-0.3 * layer_idx)
    lf = lam.astype(F32)
    lam_full = jnp.exp(jnp.sum(lf[0] * lf[1])) - jnp.exp(jnp.sum(lf[2] * lf[3])) + lam_init
    return lam_full, lam_init


def _alibi_slopes(n):
    return [2.0 ** (-8.0 * (i + 1) / n) for i in range(n)]


def _sub_rms(o, subln, out_scale):
    return o * lax.rsqrt(jnp.mean(o * o, axis=-1, keepdims=True) + RMS_EPS) * subln * out_scale


def _diffattn_prompt_kernel(q_ref, k_ref, v_ref, pos_ref, slope_ref, lam_ref, subln_ref, o_ref,
                            kaug_ref, qaug_ref, m_ref, l_ref, acc_ref, *, out_scale):
    qi = pl.program_id(2)
    tq = q_ref.shape[1]
    t_len = k_ref.shape[0]
    nrow = 2 * G_A * tq

    @pl.when(qi == 0)
    def _():
        kaug_ref[:, :LANE] = k_ref[...]
        kaug_ref[:, LANE:] = pos_ref[...]

    lane = lax.broadcasted_iota(jnp.int32, (tq, LANE), 1)
    scale = HD_A ** -0.5
    for g in range(G_A):
        qg = q_ref[g] * scale
        qaug_ref[(2 * g) * tq:(2 * g + 1) * tq, :LANE] = jnp.where(lane < HD_A, qg, 0).astype(BF16)
        qaug_ref[(2 * g + 1) * tq:(2 * g + 2) * tq, :LANE] = jnp.where(lane >= HD_A, qg, 0).astype(BF16)
    qaug_ref[:, LANE:] = slope_ref[...]

    m_ref[...] = jnp.full(m_ref.shape, NEG_INF, F32)
    l_ref[...] = jnp.zeros(l_ref.shape, F32)
    acc_ref[...] = jnp.zeros(acc_ref.shape, F32)

    def block(j, masked):
        start = pl.multiple_of(j * K_TILE, K_TILE)
        kb = kaug_ref[pl.ds(start, K_TILE), :]
        vb = v_ref[pl.ds(start, K_TILE), :]
        s = lax.dot_general(qaug_ref[...], kb, (((1,), (1,)), ((), ())), preferred_element_type=F32)
        if masked:
            row = lax.broadcasted_iota(jnp.int32, (nrow, K_TILE), 0)
            col = lax.broadcasted_iota(jnp.int32, (nrow, K_TILE), 1)
            tpos = qi * tq + (row & (tq - 1))
            s = jnp.where(start + col <= tpos, s, NEG_INF)
        m_old = m_ref[...]
        m_new = jnp.maximum(m_old, jnp.max(s, axis=-1, keepdims=True))
        alpha = jnp.exp(m_old - m_new)
        p = jnp.exp(s - m_new)
        l_ref[...] = alpha * l_ref[...] + jnp.sum(p, axis=-1, keepdims=True)
        acc_ref[...] = alpha * acc_ref[...] + jnp.dot(p.astype(BF16), vb, preferred_element_type=F32)
        m_ref[...] = m_new

    n_full = (qi * tq) // K_TILE

    def body(j, carry):
        block(j, False)
        return carry

    lax.fori_loop(0, n_full, body, 0)
    block(n_full, True)

    lam = lam_ref[...]
    norm = acc_ref[...] / l_ref[...]
    for g in range(G_A):
        o = norm[(2 * g) * tq:(2 * g + 1) * tq] - lam * norm[(2 * g + 1) * tq:(2 * g + 2) * tq]
        o_ref[:, g * LANE:(g + 1) * LANE] = _sub_rms(o, subln_ref[...], out_scale).astype(o_ref.dtype)


def _alibi_aug_consts(t_len, tq):
    pos = np.arange(t_len)
    kcols = np.zeros((t_len, LANE), np.float32)
    kcols[:, 0] = (pos // 64) * 64
    kcols[:, 1] = pos % 64
    slopes = _alibi_slopes(H_A)
    qcols = np.zeros((KV_A, 2 * G_A * tq, LANE), np.float32)
    for kv in range(KV_A):
        for g in range(G_A):
            qcols[kv, 2 * g * tq:(2 * g + 2) * tq, 0:2] = slopes[kv * G_A + g]
    return jnp.asarray(kcols, BF16), jnp.asarray(qcols, BF16)


def _diffattn_prompt(q_hm, kv_hm, lam_full, lam_init, subln, batch, t_len, nt):
    tq = Q_TILE
    nq = t_len // tq
    assert t_len % K_TILE == 0
    kcols, qcols = _alibi_aug_consts(t_len, tq)
    nrow = 2 * G_A * tq
    lam_row = jnp.full((1, LANE), lam_full, F32)
    return pl.pallas_call(
        functools.partial(_diffattn_prompt_kernel, out_scale=1.0 - lam_init),
        grid=(batch, KV_A, nq),
        in_specs=[pl.BlockSpec((G_A, tq, LANE), lambda b, kv, qi: (kv, b * nq + qi, 0)),
                  pl.BlockSpec((None, t_len, LANE), lambda b, kv, qi: (kv, b, 0)),
                  pl.BlockSpec((None, t_len, LANE), lambda b, kv, qi: (KV_A + kv, b, 0)),
                  pl.BlockSpec((t_len, LANE), lambda b, kv, qi: (0, 0)),
                  pl.BlockSpec((None, nrow, LANE), lambda b, kv, qi: (kv, 0, 0)),
                  pl.BlockSpec((1, LANE), lambda b, kv, qi: (0, 0)),
                  pl.BlockSpec((1, LANE), lambda b, kv, qi: (0, 0))],
        out_specs=pl.BlockSpec((tq, G_A * LANE), lambda b, kv, qi: (b * nq + qi, kv)),
        out_shape=jax.ShapeDtypeStruct((nt, H_A * LANE), BF16),
        scratch_shapes=[pltpu.VMEM((t_len, 2 * LANE), BF16), pltpu.VMEM((nrow, 2 * LANE), BF16),
                        pltpu.VMEM((nrow, 1), F32), pltpu.VMEM((nrow, 1), F32), pltpu.VMEM((nrow, LANE), F32)],
        compiler_params=_cparams(("arbitrary", "arbitrary", "arbitrary")),
        name="diffattn_prompt",
    )(q_hm, kv_hm, kv_hm, kcols, qcols, lam_row, subln.reshape(1, LANE))


def _diffattn_sample_kernel(pt_ref, q_ref, kn_ref, vn_ref, lam_ref, subln_ref, slope_ref, *rest,
                            out_scale, past_len):
    pp = PAGES_PER_STEP
    k_refs = rest[:pp]
    v_refs = rest[pp:2 * pp]
    o_ref, m_ref, l_ref, acc_ref = rest[2 * pp:]
    c = pl.program_id(1)
    nc = pl.num_programs(1)
    span = pp * PAGE_SIZE

    @pl.when(c == 0)
    def _():
        m_ref[...] = jnp.full(m_ref.shape, NEG_INF, F32)
        l_ref[...] = jnp.zeros(l_ref.shape, F32)
        acc_ref[...] = jnp.zeros(acc_ref.shape, F32)

    pos = (c * span + lax.broadcasted_iota(jnp.int32, (8, span), 1)).astype(F32)
    dist = past_len - pos

    for kv in range(KV_A):
        kc = jnp.concatenate([r[:, kv * LANE:(kv + 1) * LANE] for r in k_refs], axis=0).astype(BF16)
        vc = jnp.concatenate([r[:, kv * LANE:(kv + 1) * LANE] for r in v_refs], axis=0).astype(BF16)
        s = lax.dot_general(q_ref[kv].astype(BF16), kc, (((1,), (1,)), ((), ())),
                            preferred_element_type=F32)
        s = s - slope_ref[kv] * dist
        m_old = m_ref[kv]
        m_new = jnp.maximum(m_old, jnp.max(s, axis=-1, keepdims=True))
        alpha = jnp.exp(m_old - m_new)
        p = jnp.exp(s - m_new)
        l_ref[kv] = alpha * l_ref[kv] + jnp.sum(p, axis=-1, keepdims=True)
        acc_ref[kv] = alpha * acc_ref[kv] + jnp.dot(p.astype(BF16), vc, preferred_element_type=F32)
        m_ref[kv] = m_new

    @pl.when(c == nc - 1)
    def _():
        lam = lam_ref[...]
        for kv in range(KV_A):
            kn = kn_ref[kv:kv + 1, :]
            vn = vn_ref[kv:kv + 1, :]
            s_self = jnp.sum(q_ref[kv] * kn, axis=-1, keepdims=True)
            m_old = m_ref[kv]
            m_new = jnp.maximum(m_old, s_self)
            alpha = jnp.exp(m_old - m_new)
            p_self = jnp.exp(s_self - m_new)
            l_fin = alpha * l_ref[kv] + p_self
            acc = alpha * acc_ref[kv] + p_self * vn
            norm = acc / l_fin
            for g in range(G_A):
                o = norm[g:g + 1] - lam * norm[4 + g:5 + g]
                o_ref[kv * G_A + g:kv * G_A + g + 1, :] = _sub_rms(o, subln_ref[...], out_scale)


def _diffattn_sample(q_s, k_new, v_new, cache_k, cache_v, layer_j, page_table, lam_full, lam_init, subln):
    db, n_pages = page_table.shape
    qg = q_s.reshape(db, KV_A, G_A, LANE) * (HD_A ** -0.5)
    lo = jnp.arange(LANE) < HD_A
    zero = jnp.zeros((db, KV_A, 2, LANE), F32)
    q8 = jnp.concatenate([jnp.where(lo, qg, 0.0), zero, jnp.where(lo, 0.0, qg), zero], axis=2)
    pp = PAGES_PER_STEP
    assert n_pages % pp == 0
    past_len = n_pages * PAGE_SIZE
    slopes = np.zeros((KV_A, 8, 1), np.float32)
    sl = _alibi_slopes(H_A)
    for kv in range(KV_A):
        for g in range(G_A):
            slopes[kv, g, 0] = sl[kv * G_A + g]
            slopes[kv, 4 + g, 0] = sl[kv * G_A + g]
    lam_row = jnp.full((1, LANE), lam_full, F32)

    def page_spec(i):
        return pl.BlockSpec((None, None, PAGE_SIZE, KV_A * LANE),
                            lambda b, c, pt: (layer_j, pt[b, c * pp + i], 0, 0))

    vec = lambda shape: pl.BlockSpec(shape, lambda b, c, pt: (b, 0, 0))
    const2 = lambda shape: pl.BlockSpec(shape, lambda b, c, pt: (0,) * len(shape))
    in_specs = ([pl.BlockSpec((None, KV_A, 8, LANE), lambda b, c, pt: (b, 0, 0, 0)),
                 vec((None, KV_A, LANE)), vec((None, KV_A, LANE)),
                 const2((1, LANE)), const2((1, LANE)), const2((KV_A, 8, 1))]
                + [page_spec(i) for i in range(pp)] * 2)
    return pl.pallas_call(
        functools.partial(_diffattn_sample_kernel, out_scale=1.0 - lam_init, past_len=float(past_len)),
        grid_spec=pltpu.PrefetchScalarGridSpec(
            num_scalar_prefetch=1, grid=(db, n_pages // pp),
            in_specs=in_specs,
            out_specs=pl.BlockSpec((None, H_A, LANE), lambda b, c, pt: (b, 0, 0)),
            scratch_shapes=[pltpu.VMEM((KV_A, 8, 1), F32), pltpu.VMEM((KV_A, 8, 1), F32),
                            pltpu.VMEM((KV_A, 8, LANE), F32)]),
        out_shape=jax.ShapeDtypeStruct((db, H_A, LANE), F32),
        compiler_params=_cparams(("arbitrary", "arbitrary")),
        name="diffattn_sample",
    )(page_table, q8, k_new, v_new, lam_row, subln.reshape(1, LANE), jnp.asarray(slopes),
      *([cache_k] * pp), *([cache_v] * pp))


def _swa_prompt_kernel(sink_ref, q_ref, kp_ref, kc_ref, vp_ref, vc_ref, o_ref):
    n = pl.program_id(1)
    w = WINDOW
    kband = jnp.concatenate([kp_ref[...], kc_ref[...]], axis=0)
    vband = jnp.concatenate([vp_ref[...], vc_ref[...]], axis=0)
    lane = lax.broadcasted_iota(jnp.int32, (w, LANE), 1)
    row = lax.broadcasted_iota(jnp.int32, (w, 2 * w), 0)
    col = lax.broadcasted_iota(jnp.int32, (w, 2 * w), 1)
    rel = (w + row) - col
    valid = jnp.logical_and(jnp.logical_and(rel >= 0, rel <= w), jnp.logical_or(col >= w, n > 0))
    relf = rel.astype(F32)
    slopes = _alibi_slopes(H_B)
    scale = HD_B ** -0.5
    o_kv = []
    for kv in range(KV_B):
        keep = (lane < HD_B) if kv == 0 else (lane >= HD_B)
        qs = jnp.concatenate([jnp.where(keep, q_ref[p] * scale, 0).astype(BF16) for p in range(G_B)], axis=0)
        s_all = lax.dot_general(qs, kband, (((1,), (1,)), ((), ())), preferred_element_type=F32)
        ps = []
        for p in range(G_B):
            h = kv * G_B + p
            s = jnp.where(valid, s_all[p * w:(p + 1) * w] - slopes[h] * relf, NEG_INF)
            sink = sink_ref[h]
            m = jnp.maximum(jnp.max(s, axis=-1, keepdims=True), sink)
            e = jnp.exp(s - m)
            denom = jnp.sum(e, axis=-1, keepdims=True) + jnp.exp(sink - m)
            ps.append((e / denom).astype(BF16))
        o_kv.append(jnp.dot(jnp.concatenate(ps, axis=0), vband, preferred_element_type=F32))
    for p in range(G_B):
        o = jnp.where(lane < HD_B, o_kv[0][p * w:(p + 1) * w], o_kv[1][p * w:(p + 1) * w])
        o_ref[:, p * LANE:(p + 1) * LANE] = o.astype(o_ref.dtype)


def _swa_prompt(q_hm, kv_hm, sinks, batch, t_len, nt):
    w = WINDOW
    nb = t_len // w
    cur = lambda b, n, s: (b * nb + n, 0)
    prev = lambda b, n, s: (b * nb + jnp.maximum(n - 1, 0), 0)
    return pl.pallas_call(
        _swa_prompt_kernel,
        grid_spec=pltpu.PrefetchScalarGridSpec(
            num_scalar_prefetch=1, grid=(batch, nb),
            in_specs=[pl.BlockSpec((G_B, w, LANE), lambda b, n, s: (0,) + cur(b, n, s)),
                      pl.BlockSpec((None, w, LANE), lambda b, n, s: (0,) + prev(b, n, s)),
                      pl.BlockSpec((None, w, LANE), lambda b, n, s: (0,) + cur(b, n, s)),
                      pl.BlockSpec((None, w, LANE), lambda b, n, s: (1,) + prev(b, n, s)),
                      pl.BlockSpec((None, w, LANE), lambda b, n, s: (1,) + cur(b, n, s))],
            out_specs=pl.BlockSpec((w, G_B * LANE), cur)),
        out_shape=jax.ShapeDtypeStruct((nt, G_B * LANE), BF16),
        compiler_params=_cparams(("arbitrary", "arbitrary")),
        name="swa_prompt",
    )(sinks.astype(F32), q_hm, kv_hm, kv_hm, kv_hm, kv_hm)


def _swa_sample_kernel(q_ref, kb_ref, vb_ref, kn_ref, vn_ref, slope_ref, sink_ref, o_ref):
    w = WINDOW
    q = q_ref[...]
    s = lax.dot_general(q.astype(BF16), kb_ref[...].astype(BF16), (((1,), (1,)), ((), ())),
                        preferred_element_type=F32)
    dist = (w - lax.broadcasted_iota(jnp.int32, (H_B, w), 1)).astype(F32)
    s = s - slope_ref[...] * dist
    s_self = jnp.sum(q * kn_ref[...], axis=-1, keepdims=True)
    sink = sink_ref[...]
    m = jnp.maximum(jnp.maximum(jnp.max(s, axis=-1, keepdims=True), s_self), sink)
    e = jnp.exp(s - m)
    e_self = jnp.exp(s_self - m)
    denom = jnp.sum(e, axis=-1, keepdims=True) + e_self + jnp.exp(sink - m)
    o = (jnp.dot(e.astype(BF16), vb_ref[...].astype(BF16), preferred_element_type=F32)
         + e_self * vn_ref[...]) / denom
    lane = lax.broadcasted_iota(jnp.int32, (G_B, LANE), 1)
    o_ref[...] = jnp.where(lane < HD_B, o[:G_B], o[G_B:])


def _swa_sample(q_s, k_buf, v_buf, k_new, v_new, sinks):
    db = q_s.shape[0]
    lo = jnp.arange(LANE) < HD_B
    qs = q_s * (HD_B ** -0.5)
    q16 = jnp.concatenate([jnp.where(lo, qs, 0.0), jnp.where(lo, 0.0, qs)], axis=1)
    slopes = jnp.asarray(np.asarray(_alibi_slopes(H_B), np.float32).reshape(H_B, 1))
    per_b3 = lambda shape: pl.BlockSpec(shape, lambda b: (b, 0, 0))
    const = pl.BlockSpec((H_B, 1), lambda b: (0, 0))
    return pl.pallas_call(
        _swa_sample_kernel,
        grid=(db,),
        in_specs=[per_b3((None, H_B, LANE)), per_b3((None, WINDOW, LANE)), per_b3((None, WINDOW, LANE)),
                  per_b3((None, 1, LANE)), per_b3((None, 1, LANE)), const, const],
        out_specs=per_b3((None, G_B, LANE)),
        out_shape=jax.ShapeDtypeStruct((db, G_B, LANE), F32),
        compiler_params=_cparams(("arbitrary",)),
        name="swa_sample",
    )(q16, k_buf, v_buf, k_new, v_new, slopes, sinks.astype(F32).reshape(H_B, 1))


def _log_sigmoid(z):
    return jnp.minimum(z, 0.0) - jnp.log(1.0 + jnp.exp(-jnp.abs(z)))


def _silu(x):
    return x * (1.0 / (1.0 + jnp.exp(-x)))


def _gla_prompt_kernel(q_ref, k_ref, v_ref, r_ref, gt_ref, wg2_ref, bg_ref, br_ref, gn_ref,
                       o_ref, st_ref, s_ref):
    ct = pl.program_id(1)
    c = GLA_SUB
    rows_per_step = gt_ref.shape[0]

    @pl.when(ct == 0)
    def _():
        s_ref[...] = jnp.zeros(s_ref.shape, F32)

    z = jnp.dot(gt_ref[...], wg2_ref[...], preferred_element_type=F32,
                precision=lax.Precision.HIGHEST) + bg_ref[...]
    g_all = _log_sigmoid(z) * (1.0 / GATE_TAU)
    ri = lax.broadcasted_iota(jnp.int32, (c, c), 0)
    ci = lax.broadcasted_iota(jnp.int32, (c, c), 1)
    causal = ri >= ci
    tri = causal.astype(F32)
    for sub in range(rows_per_step // c):
        r0 = sub * c
        for h in range(H_C):
            g = g_all[r0:r0 + c, h * DK_C:(h + 1) * DK_C]
            bc = jnp.dot(tri, g, preferred_element_type=F32, precision=lax.Precision.HIGHEST)
            q = q_ref[h, r0:r0 + c, :] * (DK_C ** -0.5)
            k = k_ref[h, r0:r0 + c, :]
            v = jnp.concatenate([v_ref[2 * h, r0:r0 + c, :], v_ref[2 * h + 1, r0:r0 + c, :]], axis=1)
            vb = v.astype(BF16)
            q_dec = (q * jnp.exp(bc)).astype(BF16)
            k_dec = (k * jnp.exp(-bc)).astype(BF16)
            att = lax.dot_general(q_dec, k_dec, (((1,), (1,)), ((), ())), preferred_element_type=F32)
            att = jnp.where(causal, att, 0.0).astype(BF16)
            state = s_ref[h]
            o = (jnp.dot(q_dec, state.astype(BF16), preferred_element_type=F32)
                 + jnp.dot(att, vb, preferred_element_type=F32))
            b_last = bc[c - 1:c, :]
            k_tail = (k * jnp.exp(b_last - bc)).astype(BF16)
            decay_col = jnp.transpose(jnp.broadcast_to(jnp.exp(b_last), (8, DK_C)))[:, 0:1]
            s_ref[h] = decay_col * state + lax.dot_general(
                k_tail, vb, (((0,), (0,)), ((), ())), preferred_element_type=F32)
            r = jnp.concatenate([r_ref[2 * h, r0:r0 + c, :], r_ref[2 * h + 1, r0:r0 + c, :]], axis=1)
            r = _silu(r + br_ref[:, h * DV_C:(h + 1) * DV_C])
            on = o * lax.rsqrt(jnp.mean(o * o, axis=-1, keepdims=True) + RMS_EPS) * gn_ref[...]
            o_ref[r0:r0 + c, h * DV_C:(h + 1) * DV_C] = (on * r).astype(o_ref.dtype)

    @pl.when(ct == pl.num_programs(1) - 1)
    def _():
        st_ref[...] = s_ref[...]


def _gla_prompt(hm, wg2p, b_gate, b_r, gn, batch, t_len, nt):
    rt = GLA_TILE
    nc = t_len // rt
    rowblk = lambda b, ct: b * nc + ct
    const = lambda b, ct: (0, 0)
    return pl.pallas_call(
        _gla_prompt_kernel,
        grid=(batch, nc),
        in_specs=[pl.BlockSpec((H_C, rt, LANE), lambda b, ct: (0, rowblk(b, ct), 0)),
                  pl.BlockSpec((H_C, rt, LANE), lambda b, ct: (1, rowblk(b, ct), 0)),
                  pl.BlockSpec((2 * H_C, rt, LANE), lambda b, ct: (1, rowblk(b, ct), 0)),
                  pl.BlockSpec((2 * H_C, rt, LANE), lambda b, ct: (2, rowblk(b, ct), 0)),
                  pl.BlockSpec((None, rt, LANE), lambda b, ct: (6 * H_C, rowblk(b, ct), 0)),
                  pl.BlockSpec((LANE, H_C * DK_C), const), pl.BlockSpec((1, H_C * DK_C), const),
                  pl.BlockSpec((1, H_C * DV_C), const), pl.BlockSpec((1, DV_C), const)],
        out_specs=[pl.BlockSpec((rt, H_C * DV_C), lambda b, ct: (rowblk(b, ct), 0)),
                   pl.BlockSpec((None, H_C, DK_C, DV_C), lambda b, ct: (b, 0, 0, 0))],
        out_shape=[jax.ShapeDtypeStruct((nt, H_C * DV_C), BF16),
                   jax.ShapeDtypeStruct((batch, H_C, DK_C, DV_C), F32)],
        scratch_shapes=[pltpu.VMEM((H_C, DK_C, DV_C), F32)],
        compiler_params=_cparams(("arbitrary", "arbitrary")),
        name="gla_prompt",
    )(hm, hm, hm, hm, hm, wg2p, b_gate.reshape(1, -1), b_r.reshape(1, -1), gn.reshape(1, -1))


def _gla_sample_kernel(q_ref, k_ref, v_ref, r_ref, gt_ref, s_ref, wg2_ref, bg_ref, br_ref, gn_ref,
                       o_ref, so_ref):
    hi = lax.Precision.HIGHEST
    z = jnp.dot(jnp.broadcast_to(gt_ref[...], (8, LANE)), wg2_ref[...], preferred_element_type=F32,
                precision=hi)[0:1] + bg_ref[...]
    g_all = _log_sigmoid(z) * (1.0 / GATE_TAU)
    ri = lax.broadcasted_iota(jnp.int32, (DK_C, DK_C), 0)
    ci = lax.broadcasted_iota(jnp.int32, (DK_C, DK_C), 1)
    eye = ri == ci
    for h in range(H_C):
        eg = jnp.exp(g_all[:, h * DK_C:(h + 1) * DK_C])
        q = q_ref[h:h + 1, :] * (DK_C ** -0.5)
        k = k_ref[h:h + 1, :]
        v = v_ref[h:h + 1, :]
        state = s_ref[h]
        lhs = jnp.concatenate([jnp.where(eye, jnp.broadcast_to(eg, (DK_C, DK_C)), 0.0),
                               jnp.where(eye, jnp.broadcast_to(k, (DK_C, DK_C)), 0.0)], axis=1)
        rhs = jnp.concatenate([state, jnp.broadcast_to(v, (DK_C, DV_C))], axis=0)
        so_ref[h] = jnp.dot(lhs, rhs, preferred_element_type=F32, precision=hi)
        qd = q * eg
        o = (jnp.dot(jnp.broadcast_to(qd, (8, DK_C)).astype(BF16), state.astype(BF16),
                     preferred_element_type=F32)[0:1]
             + jnp.sum(q * k, axis=-1, keepdims=True) * v)
        r = _silu(r_ref[h:h + 1, :] + br_ref[:, h * DV_C:(h + 1) * DV_C])
        on = o * lax.rsqrt(jnp.mean(o * o, axis=-1, keepdims=True) + RMS_EPS) * gn_ref[...]
        o_ref[h:h + 1, :] = on * r


def _gla_sample(q_s, k_s, v_s, r_s, gt_s, state, wg2p, b_gate, b_r, gn):
    db = q_s.shape[0]
    b3 = lambda shape: pl.BlockSpec(shape, lambda b: (b, 0, 0))
    b4 = lambda shape: pl.BlockSpec(shape, lambda b: (b, 0, 0, 0))
    const = lambda shape: pl.BlockSpec(shape, lambda b: (0, 0))
    return pl.pallas_call(
        _gla_sample_kernel,
        grid=(db,),
        in_specs=[b3((None, H_C, DK_C)), b3((None, H_C, DK_C)), b3((None, H_C, DV_C)), b3((None, H_C, DV_C)),
                  b3((None, 1, LANE)), b4((None, H_C, DK_C, DV_C)),
                  const((LANE, H_C * DK_C)), const((1, H_C * DK_C)), const((1, H_C * DV_C)), const((1, DV_C))],
        out_specs=[b3((None, H_C, DV_C)), b4((None, H_C, DK_C, DV_C))],
        out_shape=[jax.ShapeDtypeStruct((db, H_C, DV_C), F32),
                   jax.ShapeDtypeStruct((db, H_C, DK_C, DV_C), F32)],
        compiler_params=_cparams(("arbitrary",)),
        name="gla_sample",
    )(q_s, k_s, v_s, r_s, gt_s, state, wg2p, b_gate.reshape(1, -1), b_r.reshape(1, -1), gn.reshape(1, -1))


def _swa_head_perm():
    perm = np.zeros((H_B * HD_B,), np.int32)
    for p in range(G_B):
        for kv in range(KV_B):
            for d in range(HD_B):
                perm[p * LANE + kv * HD_B + d] = (kv * G_B + p) * HD_B + d
    return perm


def _sample_rows(hm, np_rows, db):
    return jnp.transpose(hm[:, np_rows:np_rows + db, :], (1, 0, 2)).astype(F32)


def _with_sample_rows(attn_p, attn_s, np_rows):
    pad = attn_p.shape[0] - np_rows - attn_s.shape[0]
    tail = jnp.concatenate([attn_s.astype(attn_p.dtype), jnp.zeros((pad, attn_p.shape[1]), attn_p.dtype)], axis=0)
    return lax.dynamic_update_slice(attn_p, tail, (np_rows, 0))


def kernel(x_prompt, x_sample, cache_k_a, cache_v_a, state_swa_k, state_swa_v, state_gla, page_table,
           w_qkv_a, lam_a, subln_a, w_o_a, w_qkv_b, b_qkv_b, sinks_b, w_o_b, b_o_b,
           w_in_c, w_gate2_c, b_gate_c, b_r_c, gn_c, w_o_c, ln1_g, ln1_b, ln2_g, ln2_b,
           w_gu_d, w_down_d, w_router, w_gu_e, w_down_e):
    batch, t_len, _ = x_prompt.shape
    db = x_sample.shape[0]
    assert x_sample.shape[1] == 1 and db <= SAMPLE_PAD
    n_p = batch * t_len
    nt = n_p + SAMPLE_PAD
    n_real = n_p + db
    assert nt % TOK_TILE == 0

    y = jnp.concatenate([x_prompt.reshape(n_p, D_MODEL), x_sample.reshape(db, D_MODEL),
                         jnp.zeros((SAMPLE_PAD - db, D_MODEL), F32)], axis=0)
    yb = y.astype(BF16)
    n_pool = cache_k_a.shape[1]
    cache_k = cache_k_a.reshape(cache_k_a.shape[0], n_pool, PAGE_SIZE, KV_A * LANE)
    cache_v = cache_v_a.reshape(cache_v_a.shape[0], n_pool, PAGE_SIZE, KV_A * LANE)
    w_gu_e2 = w_gu_e.reshape((-1,) + w_gu_e.shape[2:])
    w_down_e2 = w_down_e.reshape((-1,) + w_down_e.shape[2:])
    zero_bias = jnp.zeros((D_MODEL,), F32)
    perm = _swa_head_perm()
    n_tok_tiles = nt // TOK_TILE

    ka_p, va_p, ka_s, va_s = [], [], [], []
    kb_p, vb_p, kb_s, vb_s = [], [], [], []
    gc_p, gc_s = [], []
    for i in range(DEPTH):
        j = i // 3
        if i % 3 == 0:
            nq = H_A * LANE
            (q_hm,) = _proj(yb, w_qkv_a, jnp.zeros((nq,), F32), layer=j, col0=0, ncols=nq, cw=512,
                            hm_dtype=BF16, emit_tok=False)
            kv_hm, kv_tok = _proj(yb, w_qkv_a, jnp.zeros((nq,), F32), layer=j, col0=nq, ncols=nq, cw=512,
                                  hm_dtype=BF16, emit_tok=True)
            nk = KV_A * LANE
            ka_p.append(kv_tok[:n_p, :nk].reshape(batch, t_len, KV_A, LANE))
            va_p.append(kv_tok[:n_p, nk:].reshape(batch, t_len, KV_A, LANE))
            k_new = kv_tok[n_p:n_real, :nk].reshape(db, KV_A, LANE)
            v_new = kv_tok[n_p:n_real, nk:].reshape(db, KV_A, LANE)
            ka_s.append(k_new.reshape(db, 1, KV_A, LANE))
            va_s.append(v_new.reshape(db, 1, KV_A, LANE))
            lam_full, lam_init = _diff_lambda(lam_a[j], i)
            attn_p = _diffattn_prompt(q_hm, kv_hm, lam_full, lam_init, subln_a[j], batch, t_len, nt)
            attn_s = _diffattn_sample(_sample_rows(q_hm, n_p, db), k_new, v_new, cache_k, cache_v, j,
                                      page_table, lam_full, lam_init, subln_a[j])
            attn = _with_sample_rows(attn_p, attn_s.reshape(db, H_A * LANE), n_p)
            w_o, lay_o, b_o = w_o_a, j, zero_bias
        elif i % 3 == 1:
            nq = H_B * HD_B
            w_q = w_qkv_b[j][:, :nq][:, perm][None]
            b_q = b_qkv_b[j][:nq][perm]
            w_kv = w_qkv_b[j][:, nq:][None]
            b_kv = b_qkv_b[j][nq:]
            (q_hm,) = _proj(yb, w_q, b_q, layer=0, col0=0, ncols=nq, cw=512, hm_dtype=BF16, emit_tok=False)
            kv_hm, kv_tok = _proj(yb, w_kv, b_kv, layer=0, col0=0, ncols=2 * LANE, cw=2 * LANE,
                                  hm_dtype=BF16, emit_tok=True)
            k_tok = kv_tok[:, :LANE]
            v_tok = kv_tok[:, LANE:]
            last_w = lambda a: a[:n_p].reshape(batch, t_len, LANE)[:, t_len - WINDOW:].reshape(
                batch, WINDOW, KV_B, HD_B)
            kb_p.append(last_w(k_tok))
            vb_p.append(last_w(v_tok))
            k_new = k_tok[n_p:n_real].reshape(db, 1, LANE)
            v_new = v_tok[n_p:n_real].reshape(db, 1, LANE)
            k_buf = state_swa_k[j].reshape(db, WINDOW, LANE)
            v_buf = state_swa_v[j].reshape(db, WINDOW, LANE)
            kb_s.append(jnp.concatenate([k_buf[:, 1:], k_new], axis=1).reshape(db, WINDOW, KV_B, HD_B))
            vb_s.append(jnp.concatenate([v_buf[:, 1:], v_new], axis=1).reshape(db, WINDOW, KV_B, HD_B))
            attn_p = _swa_prompt(q_hm, kv_hm, sinks_b[j], batch, t_len, nt)
            attn_s = _swa_sample(_sample_rows(q_hm, n_p, db), k_buf, v_buf, k_new, v_new, sinks_b[j])
            attn = _with_sample_rows(attn_p, attn_s.reshape(db, G_B * LANE), n_p)
            w_o, lay_o, b_o = w_o_b[j][perm][None], 0, b_o_b[j]
        else:
            n_in = w_in_c.shape[2]
            n_main = 2 * H_C * DK_C + 2 * H_C * DV_C
            w_in = jnp.zeros((1, D_MODEL, n_main + LANE), F32).at[0, :, :n_in].set(w_in_c[j])
            (hm,) = _proj(yb, w_in, jnp.zeros((n_main + LANE,), F32), layer=0, col0=0,
                          ncols=n_main + LANE, cw=5 * LANE, hm_dtype=F32, emit_tok=False)
            wg2p = jnp.zeros((LANE, H_C * DK_C), F32).at[:GATE_RANK].set(w_gate2_c[j])
            attn_p, st_p = _gla_prompt(hm, wg2p, b_gate_c[j], b_r_c[j], gn_c[j], batch, t_len, nt)
            gc_p.append(st_p)
            hs = _sample_rows(hm, n_p, db)
            q_s = hs[:, 0:H_C]
            k_s = hs[:, H_C:2 * H_C]
            v_s = hs[:, 2 * H_C:4 * H_C].reshape(db, H_C, DV_C)
            r_s = hs[:, 4 * H_C:6 * H_C].reshape(db, H_C, DV_C)
            gt_s = hs[:, 6 * H_C:6 * H_C + 1]
            attn_s, st_s = _gla_sample(q_s, k_s, v_s, r_s, gt_s, state_gla[j].astype(F32), wg2p,
                                       b_gate_c[j], b_r_c[j], gn_c[j])
            gc_s.append(st_s)
            attn = _with_sample_rows(attn_p, attn_s.reshape(db, H_C * DV_C), n_p)
            w_o, lay_o, b_o = w_o_c, j, zero_bias

        f = i // 2
        if i % 2 == 0:
            y1, y1b = _outproj_ln(attn, w_o, lay_o, b_o, y, ln1_g[i], ln1_b[i])
            y, yb = _ffn(y1b, w_gu_d, w_down_d, jnp.full((n_tok_tiles,), f, jnp.int32),
                         jnp.ones((n_tok_tiles,), jnp.int32), tm=TOK_TILE,
                         dense_args=(y1, ln2_g[i], ln2_b[i]))
        else:
            y1, y1b, route = _outproj_ln(attn, w_o, lay_o, b_o, y, ln1_g[i], ln1_b[i], w_router[f])
            y, yb = _moe(y1b, y1, route, n_real, w_gu_e2, w_down_e2, f * N_EXPERTS, ln2_g[i], ln2_b[i])

    yp = y[:n_p].reshape(batch, t_len, D_MODEL)
    ys = y[n_p:n_real].reshape(db, 1, D_MODEL)
    return (yp, ys, jnp.stack(ka_p), jnp.stack(va_p), jnp.stack(ka_s), jnp.stack(va_s),
            jnp.stack(kb_p), jnp.stack(vb_p), jnp.stack(kb_s), jnp.stack(vb_s),
            jnp.stack(gc_p), jnp.stack(gc_s))
```

```python
import functools
import math

import jax
import jax.numpy as jnp
import numpy as np
from jax import lax
from jax.experimental import pallas as pl
from jax.experimental.pallas import tpu as pltpu

F32 = jnp.float32
BF16 = jnp.bfloat16

D_MODEL = 1024
DEPTH = 4
PAGE_SIZE = 128
H_A, KV_A, G_A, HD_A = 8, 4, 2, 64
H_B, KV_B, G_B, HD_B = 16, 2, 8, 64
WINDOW = 128
H_C, DK_C, DV_C = 4, 128, 256
GATE_RANK = 16
GATE_TAU = 16.0
D_FF = 2816
N_EXPERTS = 8
LN_EPS = 1e-5
RMS_EPS = 1e-6
ALPHA = (2 * DEPTH) ** 0.25

LANE = 128
VMEM_LIMIT = 56 * 1024 * 1024
NEG_INF = float("-inf")

SAMPLE_PAD = 512
TOK_TILE = 768
FF_TILE = 256
MOE_TILE = 1024
FFN_TILE = 1056
Q_TILE = 256
K_TILE = 512
GLA_SUB = 64
GLA_TILE = 256
PAGES_PER_STEP = 16


def _cparams(sem):
    return pltpu.CompilerParams(dimension_semantics=sem, vmem_limit_bytes=VMEM_LIMIT)


def _layer_norm_rows(x, g, b):
    mu = jnp.mean(x, axis=-1, keepdims=True)
    xc = x - mu
    var = jnp.mean(xc * xc, axis=-1, keepdims=True)
    return xc * lax.rsqrt(var + LN_EPS) * g + b


def _proj_kernel(x_ref, w_ref, b_ref, *rest, emit_tok):
    if emit_tok:
        hm_ref, tok_ref, wb_ref = rest
    else:
        hm_ref, wb_ref = rest

    @pl.when(pl.program_id(1) == 0)
    def _():
        wb_ref[...] = w_ref[...].astype(BF16)

    y = jnp.dot(x_ref[...], wb_ref[...], preferred_element_type=F32) + b_ref[...]
    for c in range(hm_ref.shape[0]):
        hm_ref[c] = y[:, c * LANE:(c + 1) * LANE].astype(hm_ref.dtype)
        if emit_tok == "3d":
            tok_ref[:, c, :] = y[:, c * LANE:(c + 1) * LANE]
    if emit_tok == "2d":
        tok_ref[...] = y


def _proj(xb, w, b, *, layer, col0, ncols, cw, hm_dtype, emit_tok, tm=TOK_TILE):
    nt = xb.shape[0]
    assert ncols % cw == 0 and col0 % cw == 0 and cw % LANE == 0 and nt % tm == 0
    nb = cw // LANE
    j0 = col0 // cw
    out_shape = [jax.ShapeDtypeStruct((ncols // LANE, nt, LANE), hm_dtype)]
    out_specs = [pl.BlockSpec((nb, tm, LANE), lambda j, i: (j, i, 0))]
    if emit_tok == "3d":
        assert cw == ncols
        out_shape.append(jax.ShapeDtypeStruct((nt, nb, LANE), F32))
        out_specs.append(pl.BlockSpec((tm, nb, LANE), lambda j, i: (i, 0, 0)))
    elif emit_tok == "2d":
        out_shape.append(jax.ShapeDtypeStruct((nt, ncols), F32))
        out_specs.append(pl.BlockSpec((tm, cw), lambda j, i: (i, j)))
    return pl.pallas_call(
        functools.partial(_proj_kernel, emit_tok=emit_tok),
        grid=(ncols // cw, nt // tm),
        in_specs=[pl.BlockSpec((tm, D_MODEL), lambda j, i: (i, 0)),
                  pl.BlockSpec((None, D_MODEL, cw), lambda j, i: (layer, 0, j0 + j)),
                  pl.BlockSpec((1, cw), lambda j, i: (0, j))],
        out_specs=out_specs,
        out_shape=out_shape,
        scratch_shapes=[pltpu.VMEM((D_MODEL, cw), BF16)],
        compiler_params=_cparams(("arbitrary", "arbitrary")),
        name="proj",
    )(xb, w, b.reshape(1, ncols))


def _top2_route(logits):
    lane = lax.broadcasted_iota(jnp.int32, logits.shape, 1)
    l1 = jnp.where(lane < N_EXPERTS, logits, NEG_INF)
    m1 = jnp.max(l1, axis=-1, keepdims=True)
    i1 = jnp.min(jnp.where(l1 == m1, lane, LANE), axis=-1, keepdims=True)
    l2 = jnp.where(lane == i1, NEG_INF, l1)
    m2 = jnp.max(l2, axis=-1, keepdims=True)
    i2 = jnp.min(jnp.where(l2 == m2, lane, LANE), axis=-1, keepdims=True)
    e = jnp.exp(m2 - m1)
    g1 = 1.0 / (1.0 + e)
    g2 = e / (1.0 + e)
    return jnp.where(lane == 0, g1,
                     jnp.where(lane == 1, g2,
                               jnp.where(lane == 2, i1.astype(F32),
                                         jnp.where(lane == 3, i2.astype(F32), 0.0))))


def _outproj_kernel(ap_ref, as_ref, w_ref, b_ref, res_ref, g_ref, be_ref, *rest, route):
    if route:
        wr_ref, y_ref, yb_ref, rt_ref, wb_ref = rest
    else:
        y_ref, yb_ref, wb_ref = rest

    @pl.when(pl.program_id(0) == 0)
    def _():
        wb_ref[...] = w_ref[...].astype(BF16)

    a = jnp.where(pl.program_id(0) == pl.num_programs(0) - 1, as_ref[...], ap_ref[...])
    m = jnp.dot(a, wb_ref[...], preferred_element_type=F32) + b_ref[...]
    y = _layer_norm_rows(ALPHA * res_ref[...] + m, g_ref[...], be_ref[...])
    y_ref[...] = y
    yb_ref[...] = y.astype(BF16)
    if route:
        logits = jnp.dot(y, wr_ref[...], preferred_element_type=F32, precision=lax.Precision.HIGHEST)
        rt_ref[...] = _top2_route(logits)


def _outproj_ln(a_p, a_s, w, layer, b, res, g, be, w_router=None):
    tm = SAMPLE_PAD
    n_p, k = a_p.shape
    nt = res.shape[0]
    assert n_p % tm == 0 and nt == n_p + tm
    a_s = jnp.concatenate([a_s.astype(BF16), jnp.zeros((tm - a_s.shape[0], k), BF16)], axis=0)
    n_ptiles = n_p // tm
    route = w_router is not None
    row = lambda i: (i, 0)
    const = lambda i: (0, 0)
    in_specs = [pl.BlockSpec((tm, k), lambda i: (jnp.minimum(i, n_ptiles - 1), 0)),
                pl.BlockSpec((tm, k), const), pl.BlockSpec((None, k, D_MODEL), lambda i: (layer, 0, 0)),
                pl.BlockSpec((1, D_MODEL), const), pl.BlockSpec((tm, D_MODEL), row),
                pl.BlockSpec((1, D_MODEL), const), pl.BlockSpec((1, D_MODEL), const)]
    args = [a_p, a_s, w, b.reshape(1, D_MODEL), res, g.reshape(1, D_MODEL), be.reshape(1, D_MODEL)]
    out_shape = [jax.ShapeDtypeStruct((nt, D_MODEL), F32), jax.ShapeDtypeStruct((nt, D_MODEL), BF16)]
    out_specs = [pl.BlockSpec((tm, D_MODEL), row), pl.BlockSpec((tm, D_MODEL), row)]
    if route:
        wr = jnp.zeros((D_MODEL, LANE), F32).at[:, :N_EXPERTS].set(w_router)
        in_specs.append(pl.BlockSpec((D_MODEL, LANE), const))
        args.append(wr)
        out_shape.append(jax.ShapeDtypeStruct((nt, LANE), F32))
        out_specs.append(pl.BlockSpec((tm, LANE), row))
    return pl.pallas_call(
        functools.partial(_outproj_kernel, route=route),
        grid=(nt // tm,),
        in_specs=in_specs, out_specs=out_specs, out_shape=out_shape,
        scratch_shapes=[pltpu.VMEM((k, D_MODEL), BF16)],
        compiler_params=_cparams(("arbitrary",)),
        name="outproj_ln",
    )(*args)


def _ffn_kernel(te_ref, tv_ref, x_ref, wg_ref, wu_ref, wd_ref, *rest, dense):
    if dense:
        res_ref, g_ref, be_ref, y_ref, yb_ref, acc_ref = rest
    else:
        y_ref, acc_ref, xb_ref = rest
    i = pl.program_id(0)
    j = pl.program_id(1)
    nj = pl.num_programs(1)

    @pl.when(tv_ref[i] > 0)
    def _():
        if dense:
            x = x_ref[...]
        else:
            @pl.when(j == 0)
            def _():
                xb_ref[...] = x_ref[...].astype(BF16)
            x = xb_ref[...]
        gate = jnp.dot(x, wg_ref[...].astype(BF16), preferred_element_type=F32)
        up = jnp.dot(x, wu_ref[...].astype(BF16), preferred_element_type=F32)
        h = (gate * (1.0 / (1.0 + jnp.exp(-gate))) * up).astype(BF16)
        part = jnp.dot(h, wd_ref[...].astype(BF16), preferred_element_type=F32)

        @pl.when(j == 0)
        def _():
            acc_ref[...] = part

        @pl.when(j > 0)
        def _():
            acc_ref[...] += part

        @pl.when(j == nj - 1)
        def _():
            if dense:
                y = _layer_norm_rows(ALPHA * res_ref[...] + acc_ref[...], g_ref[...], be_ref[...])
                y_ref[...] = y
                yb_ref[...] = y.astype(BF16)
            else:
                y_ref[...] = acc_ref[...]

    if not dense:
        @pl.when(jnp.logical_and(tv_ref[i] == 0, j == nj - 1))
        def _():
            y_ref[...] = jnp.zeros_like(y_ref)


def _ffn(xb, w_gu, w_down, tile_expert, tile_valid, *, tm, dense_args=None, tf=FF_TILE):
    m = xb.shape[0]
    assert m % tm == 0 and D_FF % tf == 0
    nf = D_FF // tf
    dense = dense_args is not None

    def jeff(i, j, tv):
        return jnp.where(tv[i] > 0, j, nf - 1)

    in_specs = [
        pl.BlockSpec((tm, D_MODEL), lambda i, j, te, tv: (i, 0)),
        pl.BlockSpec((None, D_MODEL, tf), lambda i, j, te, tv: (te[i], 0, jeff(i, j, tv))),
        pl.BlockSpec((None, D_MODEL, tf), lambda i, j, te, tv: (te[i], 0, nf + jeff(i, j, tv))),
        pl.BlockSpec((None, tf, D_MODEL), lambda i, j, te, tv: (te[i], jeff(i, j, tv), 0)),
    ]
    args = [xb, w_gu, w_gu, w_down]
    row = lambda i, j, te, tv: (i, 0)
    const = lambda i, j, te, tv: (0, 0)
    if dense:
        res, g, be = dense_args
        in_specs += [pl.BlockSpec((tm, D_MODEL), row), pl.BlockSpec((1, D_MODEL), const),
                     pl.BlockSpec((1, D_MODEL), const)]
        args += [res, g.reshape(1, D_MODEL), be.reshape(1, D_MODEL)]
        out_shape = [jax.ShapeDtypeStruct((m, D_MODEL), F32), jax.ShapeDtypeStruct((m, D_MODEL), BF16)]
        out_specs = [pl.BlockSpec((tm, D_MODEL), row), pl.BlockSpec((tm, D_MODEL), row)]
    else:
        out_shape = jax.ShapeDtypeStruct((m, D_MODEL), F32)
        out_specs = pl.BlockSpec((tm, D_MODEL), row)
    scratch = [pltpu.VMEM((tm, D_MODEL), F32)] + ([] if dense else [pltpu.VMEM((tm, D_MODEL), BF16)])
    return pl.pallas_call(
        functools.partial(_ffn_kernel, dense=dense),
        grid_spec=pltpu.PrefetchScalarGridSpec(
            num_scalar_prefetch=2, grid=(m // tm, nf),
            in_specs=in_specs, out_specs=out_specs, scratch_shapes=scratch),
        out_shape=out_shape,
        compiler_params=_cparams(("arbitrary", "arbitrary")),
        name="ffn_dense" if dense else "ffn_moe",
    )(tile_expert, tile_valid, *args)


def _combine_kernel(ya_ref, yb_ref, rt_ref, res_ref, g_ref, be_ref, y_ref, ybf_ref):
    rt = rt_ref[...]
    f = rt[:, 0:1] * ya_ref[...] + rt[:, 1:2] * yb_ref[...]
    y = _layer_norm_rows(ALPHA * res_ref[...] + f, g_ref[...], be_ref[...])
    y_ref[...] = y
    ybf_ref[...] = y.astype(BF16)


def _combine_ln(ya, yb, route, res, g, be, *, tm=TOK_TILE):
    nt = res.shape[0]
    row = lambda i: (i, 0)
    const = lambda i: (0, 0)
    return pl.pallas_call(
        _combine_kernel,
        grid=(nt // tm,),
        in_specs=[pl.BlockSpec((tm, D_MODEL), row), pl.BlockSpec((tm, D_MODEL), row),
                  pl.BlockSpec((tm, LANE), row), pl.BlockSpec((tm, D_MODEL), row),
                  pl.BlockSpec((1, D_MODEL), const), pl.BlockSpec((1, D_MODEL), const)],
        out_specs=[pl.BlockSpec((tm, D_MODEL), row), pl.BlockSpec((tm, D_MODEL), row)],
        out_shape=[jax.ShapeDtypeStruct((nt, D_MODEL), F32), jax.ShapeDtypeStruct((nt, D_MODEL), BF16)],
        compiler_params=_cparams(("arbitrary",)),
        name="moe_combine_ln",
    )(ya, yb, route, res, g.reshape(1, D_MODEL), be.reshape(1, D_MODEL))


def _moe(res, route, n_real, w_gu, w_down, e_off, g, be, *, tm=MOE_TILE):
    nt = res.shape[0]
    gates_idx = route[:n_real, 2:4].astype(jnp.int32)
    flat_e = gates_idx.reshape(-1)
    npair = flat_e.shape[0]
    onehot = (flat_e[:, None] == jnp.arange(N_EXPERTS)[None, :]).astype(jnp.int32)
    csum = jnp.cumsum(onehot, axis=0)
    counts = csum[-1]
    rank = jnp.take_along_axis(csum, flat_e[:, None], axis=1)[:, 0] - 1
    tiles_per = (counts + tm - 1) // tm
    tile_end = jnp.cumsum(tiles_per)
    tile_start = tile_end - tiles_per
    n_tiles = npair // tm + N_EXPERTS
    m_pad = n_tiles * tm
    dest = tile_start[flat_e] * tm + rank
    row_src = jnp.zeros((m_pad,), jnp.int32).at[dest].set(jnp.arange(npair, dtype=jnp.int32) // 2)
    t_ids = jnp.arange(n_tiles, dtype=jnp.int32)
    tile_valid = (t_ids < tile_end[-1]).astype(jnp.int32)
    tile_expert = jnp.sum((t_ids[:, None] >= tile_end[None, :]).astype(jnp.int32), axis=1)
    tile_expert = jnp.minimum(tile_expert, N_EXPERTS - 1)
    last_e = tile_expert[jnp.maximum(tile_end[-1] - 1, 0)]
    tile_expert = jnp.where(tile_valid > 0, tile_expert, last_e) + e_off
    x_sorted = jnp.take(res, row_src, axis=0)
    y_sorted = _ffn(x_sorted, w_gu, w_down, tile_expert, tile_valid, tm=tm)
    dest2 = jnp.zeros((nt, 2), jnp.int32).at[:n_real].set(dest.reshape(n_real, 2))
    ya = jnp.take(y_sorted, dest2[:, 0], axis=0)
    ybb = jnp.take(y_sorted, dest2[:, 1], axis=0)
    return _combine_ln(ya, ybb, route, res, g, be)


def _diff_lambda(lam, layer_idx):
    lam_init = 0.8 - 0.6 * math.exp(-0.3 * layer_idx)
    lf = lam.astype(F32)
    lam_full = jnp.exp(jnp.sum(lf[0] * lf[1])) - jnp.exp(jnp.sum(lf[2] * lf[3])) + lam_init
    return lam_full, lam_init


def _alibi_slopes(n):
    return [2.0 ** (-8.0 * (i + 1) / n) for i in range(n)]


def _sub_rms(o, subln, out_scale):
    return o * lax.rsqrt(jnp.mean(o * o, axis=-1, keepdims=True) + RMS_EPS) * subln * out_scale


def _diffattn_prompt_kernel(q_ref, k_ref, v_ref, pos_ref, slope_ref, lam_ref, subln_ref, o_ref,
                            kaug_ref, qaug_ref, m_ref, l_ref, acc_ref, *, out_scale):
    qi = pl.program_id(2)
    tq = q_ref.shape[1]
    t_len = k_ref.shape[0]
    nrow = 2 * G_A * tq

    @pl.when(qi == 0)
    def _():
        kaug_ref[:, :LANE] = k_ref[...]
        kaug_ref[:, LANE:] = pos_ref[...]

    lane = lax.broadcasted_iota(jnp.int32, (tq, LANE), 1)
    scale = HD_A ** -0.5
    for g in range(G_A):
        qg = q_ref[g] * scale
        qaug_ref[(2 * g) * tq:(2 * g + 1) * tq, :LANE] = jnp.where(lane < HD_A, qg, 0).astype(BF16)
        qaug_ref[(2 * g + 1) * tq:(2 * g + 2) * tq, :LANE] = jnp.where(lane >= HD_A, qg, 0).astype(BF16)
    qaug_ref[:, LANE:] = slope_ref[...]

    m_ref[...] = jnp.full(m_ref.shape, NEG_INF, F32)
    l_ref[...] = jnp.zeros(l_ref.shape, F32)
    acc_ref[...] = jnp.zeros(acc_ref.shape, F32)

    def block(j, masked):
        start = pl.multiple_of(j * K_TILE, K_TILE)
        kb = kaug_ref[pl.ds(start, K_TILE), :]
        vb = v_ref[pl.ds(start, K_TILE), :]
        s = lax.dot_general(qaug_ref[...], kb, (((1,), (1,)), ((), ())), preferred_element_type=F32)
        if masked:
            row = lax.broadcasted_iota(jnp.int32, (nrow, K_TILE), 0)
            col = lax.broadcasted_iota(jnp.int32, (nrow, K_TILE), 1)
            s = jnp.where(start + col <= qi * tq + (row & (tq - 1)), s, NEG_INF)
        m_old = m_ref[...]
        m_new = jnp.maximum(m_old, jnp.max(s, axis=-1, keepdims=True))
        alpha = jnp.exp(m_old - m_new)
        p = jnp.exp(s - m_new)
        l_ref[...] = alpha * l_ref[...] + jnp.sum(p, axis=-1, keepdims=True)
        acc_ref[...] = alpha * acc_ref[...] + jnp.dot(p.astype(BF16), vb, preferred_element_type=F32)
        m_ref[...] = m_new

    n_full = (qi * tq) // K_TILE

    def body(j, carry):
        block(j, False)
        return carry

    lax.fori_loop(0, n_full, body, 0)
    block(n_full, True)

    lam = lam_ref[...]
    norm = acc_ref[...] / l_ref[...]
    for g in range(G_A):
        o = norm[(2 * g) * tq:(2 * g + 1) * tq] - lam * norm[(2 * g + 1) * tq:(2 * g + 2) * tq]
        o_ref[:, g * LANE:(g + 1) * LANE] = _sub_rms(o, subln_ref[...], out_scale).astype(o_ref.dtype)


def _alibi_aug_consts(t_len, tq):
    pos = np.arange(t_len)
    kcols = np.zeros((t_len, LANE), np.float32)
    kcols[:, 0] = (pos // 64) * 64
    kcols[:, 1] = pos % 64
    slopes = _alibi_slopes(H_A)
    qcols = np.zeros((KV_A, 2 * G_A * tq, LANE), np.float32)
    for kv in range(KV_A):
        for g in range(G_A):
            qcols[kv, 2 * g * tq:(2 * g + 2) * tq, 0:2] = slopes[kv * G_A + g]
    return jnp.asarray(kcols, BF16), jnp.asarray(qcols, BF16)


def _diffattn_prompt(q_hm, kv_hm, lam_full, lam_init, subln, batch, t_len):
    tq = Q_TILE
    nq = t_len // tq
    assert t_len % K_TILE == 0
    kcols, qcols = _alibi_aug_consts(t_len, tq)
    nrow = 2 * G_A * tq
    lam_row = jnp.full((1, LANE), lam_full, F32)
    return pl.pallas_call(
        functools.partial(_diffattn_prompt_kernel, out_scale=1.0 - lam_init),
        grid=(batch, KV_A, nq),
        in_specs=[pl.BlockSpec((G_A, tq, LANE), lambda b, kv, qi: (kv, b * nq + qi, 0)),
                  pl.BlockSpec((None, t_len, LANE), lambda b, kv, qi: (kv, b, 0)),
                  pl.BlockSpec((None, t_len, LANE), lambda b, kv, qi: (KV_A + kv, b, 0)),
                  pl.BlockSpec((t_len, LANE), lambda b, kv, qi: (0, 0)),
                  pl.BlockSpec((None, nrow, LANE), lambda b, kv, qi: (kv, 0, 0)),
                  pl.BlockSpec((1, LANE), lambda b, kv, qi: (0, 0)),
                  pl.BlockSpec((1, LANE), lambda b, kv, qi: (0, 0))],
        out_specs=pl.BlockSpec((tq, G_A * LANE), lambda b, kv, qi: (b * nq + qi, kv)),
        out_shape=jax.ShapeDtypeStruct((batch * t_len, H_A * LANE), BF16),
        scratch_shapes=[pltpu.VMEM((t_len, 2 * LANE), BF16), pltpu.VMEM((nrow, 2 * LANE), BF16),
                        pltpu.VMEM((nrow, 1), F32), pltpu.VMEM((nrow, 1), F32), pltpu.VMEM((nrow, LANE), F32)],
        compiler_params=_cparams(("arbitrary", "arbitrary", "arbitrary")),
        name="diffattn_prompt",
    )(q_hm, kv_hm, kv_hm, kcols, qcols, lam_row, subln.reshape(1, LANE))


def _diffattn_sample_kernel(pt_ref, q_ref, kn_ref, vn_ref, lam_ref, subln_ref, slope_ref, *rest,
                            out_scale, past_len):
    pp = PAGES_PER_STEP
    k_refs = rest[:pp]
    v_refs = rest[pp:2 * pp]
    o_ref, m_ref, l_ref, acc_ref = rest[2 * pp:]
    c = pl.program_id(1)
    nc = pl.num_programs(1)
    nrow = KV_A * 8
    ncol = pp * PAGE_SIZE * KV_A

    @pl.when(c == 0)
    def _():
        m_ref[...] = jnp.full(m_ref.shape, NEG_INF, F32)
        l_ref[...] = jnp.zeros(l_ref.shape, F32)
        acc_ref[...] = jnp.zeros(acc_ref.shape, F32)

    kc = jnp.concatenate([r[...] for r in k_refs], axis=0).astype(BF16)
    vc = jnp.concatenate([r[...] for r in v_refs], axis=0).astype(BF16)
    s = lax.dot_general(q_ref[...].astype(BF16), kc, (((1,), (1,)), ((), ())), preferred_element_type=F32)
    row = lax.broadcasted_iota(jnp.int32, (nrow, ncol), 0)
    col = lax.broadcasted_iota(jnp.int32, (nrow, ncol), 1)
    pos = c * (pp * PAGE_SIZE) + (col >> 2)
    dist = past_len - pos.astype(F32)
    s = jnp.where((col & (KV_A - 1)) == (row >> 3), s - slope_ref[...] * dist, NEG_INF)
    m_old = m_ref[...]
    m_new = jnp.maximum(m_old, jnp.max(s, axis=-1, keepdims=True))
    alpha = jnp.exp(m_old - m_new)
    p = jnp.exp(s - m_new)
    l_ref[...] = alpha * l_ref[...] + jnp.sum(p, axis=-1, keepdims=True)
    acc_ref[...] = alpha * acc_ref[...] + jnp.dot(p.astype(BF16), vc, preferred_element_type=F32)
    m_ref[...] = m_new

    @pl.when(c == nc - 1)
    def _():
        lam = lam_ref[...]
        s_self = jnp.sum(q_ref[...] * kn_ref[...], axis=-1, keepdims=True)
        m_old = m_ref[...]
        m_fin = jnp.maximum(m_old, s_self)
        alpha = jnp.exp(m_old - m_fin)
        p_self = jnp.exp(s_self - m_fin)
        norm = (alpha * acc_ref[...] + p_self * vn_ref[...]) / (alpha * l_ref[...] + p_self)
        for kv in range(KV_A):
            for g in range(G_A):
                o = norm[kv * 8 + g:kv * 8 + g + 1] - lam * norm[kv * 8 + 4 + g:kv * 8 + 5 + g]
                o_ref[kv * G_A + g:kv * G_A + g + 1, :] = _sub_rms(o, subln_ref[...], out_scale)


def _diffattn_sample(q_s, k_new, v_new, cache_k, cache_v, layer_j, page_table, lam_full, lam_init, subln):
    db, n_pages = page_table.shape
    nrow = KV_A * 8
    qg = q_s.reshape(db, KV_A, G_A, LANE) * (HD_A ** -0.5)
    lo = jnp.arange(LANE) < HD_A
    zero = jnp.zeros((db, KV_A, 2, LANE), F32)
    q32 = jnp.concatenate([jnp.where(lo, qg, 0.0), zero, jnp.where(lo, 0.0, qg), zero], axis=2)
    q32 = q32.reshape(db, nrow, LANE)
    kn32 = jnp.repeat(k_new, 8, axis=1)
    vn32 = jnp.repeat(v_new, 8, axis=1)
    pp = PAGES_PER_STEP
    assert n_pages % pp == 0 and KV_A == 4
    past_len = n_pages * PAGE_SIZE
    slopes = np.zeros((nrow, 1), np.float32)
    sl = _alibi_slopes(H_A)
    for kv in range(KV_A):
        for g in range(G_A):
            slopes[kv * 8 + g, 0] = sl[kv * G_A + g]
            slopes[kv * 8 + 4 + g, 0] = sl[kv * G_A + g]
    lam_row = jnp.full((1, LANE), lam_full, F32)

    def page_spec(i):
        return pl.BlockSpec((None, None, PAGE_SIZE * KV_A, LANE),
                            lambda b, c, pt: (layer_j, pt[b, c * pp + i], 0, 0))

    vec = pl.BlockSpec((None, nrow, LANE), lambda b, c, pt: (b, 0, 0))
    const2 = lambda shape: pl.BlockSpec(shape, lambda b, c, pt: (0, 0))
    in_specs = ([vec, vec, vec, const2((1, LANE)), const2((1, LANE)), const2((nrow, 1))]
                + [page_spec(i) for i in range(pp)] * 2)
    return pl.pallas_call(
        functools.partial(_diffattn_sample_kernel, out_scale=1.0 - lam_init, past_len=float(past_len)),
        grid_spec=pltpu.PrefetchScalarGridSpec(
            num_scalar_prefetch=1, grid=(db, n_pages // pp),
            in_specs=in_specs,
            out_specs=pl.BlockSpec((None, H_A, LANE), lambda b, c, pt: (b, 0, 0)),
            scratch_shapes=[pltpu.VMEM((nrow, 1), F32), pltpu.VMEM((nrow, 1), F32),
                            pltpu.VMEM((nrow, LANE), F32)]),
        out_shape=jax.ShapeDtypeStruct((db, H_A, LANE), F32),
        compiler_params=_cparams(("arbitrary", "arbitrary")),
        name="diffattn_sample",
    )(page_table, q32, kn32, vn32, lam_row, subln.reshape(1, LANE), jnp.asarray(slopes),
      *([cache_k] * pp), *([cache_v] * pp))


def _swa_prompt_kernel(sink_ref, q_ref, kp_ref, kc_ref, vp_ref, vc_ref, o_ref):
    n = pl.program_id(1)
    w = WINDOW
    kband = jnp.concatenate([kp_ref[...], kc_ref[...]], axis=0)
    vband = jnp.concatenate([vp_ref[...], vc_ref[...]], axis=0)
    lane = lax.broadcasted_iota(jnp.int32, (w, LANE), 1)
    row = lax.broadcasted_iota(jnp.int32, (w, 2 * w), 0)
    col = lax.broadcasted_iota(jnp.int32, (w, 2 * w), 1)
    rel = (w + row) - col
    valid = jnp.logical_and(jnp.logical_and(rel >= 0, rel <= w), jnp.logical_or(col >= w, n > 0))
    relf = rel.astype(F32)
    slopes = _alibi_slopes(H_B)
    scale = HD_B ** -0.5
    o_kv = []
    for kv in range(KV_B):
        keep = (lane < HD_B) if kv == 0 else (lane >= HD_B)
        qs = jnp.concatenate([jnp.where(keep, q_ref[p] * scale, 0).astype(BF16) for p in range(G_B)], axis=0)
        s_all = lax.dot_general(qs, kband, (((1,), (1,)), ((), ())), preferred_element_type=F32)
        ps = []
        for p in range(G_B):
            h = kv * G_B + p
            s = jnp.where(valid, s_all[p * w:(p + 1) * w] - slopes[h] * relf, NEG_INF)
            sink = sink_ref[h]
            m = jnp.maximum(jnp.max(s, axis=-1, keepdims=True), sink)
            e = jnp.exp(s - m)
            denom = jnp.sum(e, axis=-1, keepdims=True) + jnp.exp(sink - m)
            ps.append((e / denom).astype(BF16))
        o_kv.append(jnp.dot(jnp.concatenate(ps, axis=0), vband, preferred_element_type=F32))
    for p in range(G_B):
        o = jnp.where(lane < HD_B, o_kv[0][p * w:(p + 1) * w], o_kv[1][p * w:(p + 1) * w])
        o_ref[:, p * LANE:(p + 1) * LANE] = o.astype(o_ref.dtype)


def _swa_prompt(q_hm, kv_hm, sinks, batch, t_len):
    w = WINDOW
    nb = t_len // w
    cur = lambda b, n, s: (b * nb + n, 0)
    prev = lambda b, n, s: (b * nb + jnp.maximum(n - 1, 0), 0)
    return pl.pallas_call(
        _swa_prompt_kernel,
        grid_spec=pltpu.PrefetchScalarGridSpec(
            num_scalar_prefetch=1, grid=(batch, nb),
            in_specs=[pl.BlockSpec((G_B, w, LANE), lambda b, n, s: (0,) + cur(b, n, s)),
                      pl.BlockSpec((None, w, LANE), lambda b, n, s: (0,) + prev(b, n, s)),
                      pl.BlockSpec((None, w, LANE), lambda b, n, s: (0,) + cur(b, n, s)),
                      pl.BlockSpec((None, w, LANE), lambda b, n, s: (1,) + prev(b, n, s)),
                      pl.BlockSpec((None, w, LANE), lambda b, n, s: (1,) + cur(b, n, s))],
            out_specs=pl.BlockSpec((w, G_B * LANE), cur)),
        out_shape=jax.ShapeDtypeStruct((batch * t_len, G_B * LANE), BF16),
        compiler_params=_cparams(("arbitrary", "arbitrary")),
        name="swa_prompt",
    )(sinks.astype(F32), q_hm, kv_hm, kv_hm, kv_hm, kv_hm)


def _swa_sample_kernel(q_ref, kb_ref, vb_ref, kn_ref, vn_ref, slope_ref, sink_ref, o_ref):
    w = WINDOW
    q = q_ref[...]
    s = lax.dot_general(q.astype(BF16), kb_ref[...].astype(BF16), (((1,), (1,)), ((), ())),
                        preferred_element_type=F32)
    dist = (w - lax.broadcasted_iota(jnp.int32, (H_B, w), 1)).astype(F32)
    s = s - slope_ref[...] * dist
    s_self = jnp.sum(q * kn_ref[...], axis=-1, keepdims=True)
    sink = sink_ref[...]
    m = jnp.maximum(jnp.maximum(jnp.max(s, axis=-1, keepdims=True), s_self), sink)
    e = jnp.exp(s - m)
    e_self = jnp.exp(s_self - m)
    denom = jnp.sum(e, axis=-1, keepdims=True) + e_self + jnp.exp(sink - m)
    o = (jnp.dot(e.astype(BF16), vb_ref[...].astype(BF16), preferred_element_type=F32)
         + e_self * vn_ref[...]) / denom
    lane = lax.broadcasted_iota(jnp.int32, (G_B, LANE), 1)
    o_ref[...] = jnp.where(lane < HD_B, o[:G_B], o[G_B:])


def _swa_sample(q_s, k_buf, v_buf, k_new, v_new, sinks):
    db = q_s.shape[0]
    lo = jnp.arange(LANE) < HD_B
    qs = q_s * (HD_B ** -0.5)
    q16 = jnp.concatenate([jnp.where(lo, qs, 0.0), jnp.where(lo, 0.0, qs)], axis=1)
    slopes = jnp.asarray(np.asarray(_alibi_slopes(H_B), np.float32).reshape(H_B, 1))
    per_b3 = lambda shape: pl.BlockSpec(shape, lambda b: (b, 0, 0))
    const = pl.BlockSpec((H_B, 1), lambda b: (0, 0))
    return pl.pallas_call(
        _swa_sample_kernel,
        grid=(db,),
        in_specs=[per_b3((None, H_B, LANE)), per_b3((None, WINDOW, LANE)), per_b3((None, WINDOW, LANE)),
                  per_b3((None, 1, LANE)), per_b3((None, 1, LANE)), const, const],
        out_specs=per_b3((None, G_B, LANE)),
        out_shape=jax.ShapeDtypeStruct((db, G_B, LANE), F32),
        compiler_params=_cparams(("arbitrary",)),
        name="swa_sample",
    )(q16, k_buf, v_buf, k_new, v_new, slopes, sinks.astype(F32).reshape(H_B, 1))


def _log_sigmoid(z):
    return jnp.minimum(z, 0.0) - jnp.log(1.0 + jnp.exp(-jnp.abs(z)))


def _silu(x):
    return x * (1.0 / (1.0 + jnp.exp(-x)))


def _gla_prompt_kernel(q_ref, k_ref, v_ref, r_ref, gt_ref, wg2_ref, bg_ref, br_ref, gn_ref,
                       o_ref, st_ref, s_ref):
    ct = pl.program_id(1)
    c = GLA_SUB
    rows_per_step = gt_ref.shape[0]

    @pl.when(ct == 0)
    def _():
        s_ref[...] = jnp.zeros(s_ref.shape, F32)

    z = jnp.dot(gt_ref[...], wg2_ref[...], preferred_element_type=F32,
                precision=lax.Precision.HIGHEST) + bg_ref[...]
    g_all = _log_sigmoid(z) * (1.0 / GATE_TAU)
    ri = lax.broadcasted_iota(jnp.int32, (c, c), 0)
    ci = lax.broadcasted_iota(jnp.int32, (c, c), 1)
    causal = ri >= ci
    tri = causal.astype(F32)
    for sub in range(rows_per_step // c):
        r0 = sub * c
        for h in range(H_C):
            g = g_all[r0:r0 + c, h * DK_C:(h + 1) * DK_C]
            bc = jnp.dot(tri, g, preferred_element_type=F32, precision=lax.Precision.HIGHEST)
            q = q_ref[h, r0:r0 + c, :] * (DK_C ** -0.5)
            k = k_ref[h, r0:r0 + c, :]
            v = jnp.concatenate([v_ref[2 * h, r0:r0 + c, :], v_ref[2 * h + 1, r0:r0 + c, :]], axis=1)
            vb = v.astype(BF16)
            q_dec = (q * jnp.exp(bc)).astype(BF16)
            k_dec = (k * jnp.exp(-bc)).astype(BF16)
            att = lax.dot_general(q_dec, k_dec, (((1,), (1,)), ((), ())), preferred_element_type=F32)
            att = jnp.where(causal, att, 0.0).astype(BF16)
            state = s_ref[h]
            o = (jnp.dot(q_dec, state.astype(BF16), preferred_element_type=F32)
                 + jnp.dot(att, vb, preferred_element_type=F32))
            b_last = bc[c - 1:c, :]
            k_tail = (k * jnp.exp(b_last - bc)).astype(BF16)
            decay_col = jnp.transpose(jnp.broadcast_to(jnp.exp(b_last), (8, DK_C)))[:, 0:1]
            s_ref[h] = decay_col * state + lax.dot_general(
                k_tail, vb, (((0,), (0,)), ((), ())), preferred_element_type=F32)
            r = jnp.concatenate([r_ref[2 * h, r0:r0 + c, :], r_ref[2 * h + 1, r0:r0 + c, :]], axis=1)
            r = _silu(r + br_ref[:, h * DV_C:(h + 1) * DV_C])
            on = o * lax.rsqrt(jnp.mean(o * o, axis=-1, keepdims=True) + RMS_EPS) * gn_ref[...]
            o_ref[r0:r0 + c, h * DV_C:(h + 1) * DV_C] = (on * r).astype(o_ref.dtype)

    @pl.when(ct == pl.num_programs(1) - 1)
    def _():
        st_ref[...] = s_ref[...]


def _gla_prompt(hm, wg2p, b_gate, b_r, gn, batch, t_len):
    rt = GLA_TILE
    nc = t_len // rt
    rowblk = lambda b, ct: b * nc + ct
    const = lambda b, ct: (0, 0)
    return pl.pallas_call(
        _gla_prompt_kernel,
        grid=(batch, nc),
        in_specs=[pl.BlockSpec((H_C, rt, LANE), lambda b, ct: (0, rowblk(b, ct), 0)),
                  pl.BlockSpec((H_C, rt, LANE), lambda b, ct: (1, rowblk(b, ct), 0)),
                  pl.BlockSpec((2 * H_C, rt, LANE), lambda b, ct: (1, rowblk(b, ct), 0)),
                  pl.BlockSpec((2 * H_C, rt, LANE), lambda b, ct: (2, rowblk(b, ct), 0)),
                  pl.BlockSpec((None, rt, LANE), lambda b, ct: (6 * H_C, rowblk(b, ct), 0)),
                  pl.BlockSpec((LANE, H_C * DK_C), const), pl.BlockSpec((1, H_C * DK_C), const),
                  pl.BlockSpec((1, H_C * DV_C), const), pl.BlockSpec((1, DV_C), const)],
        out_specs=[pl.BlockSpec((rt, H_C * DV_C), lambda b, ct: (rowblk(b, ct), 0)),
                   pl.BlockSpec((None, H_C, DK_C, DV_C), lambda b, ct: (b, 0, 0, 0))],
        out_shape=[jax.ShapeDtypeStruct((batch * t_len, H_C * DV_C), BF16),
                   jax.ShapeDtypeStruct((batch, H_C, DK_C, DV_C), F32)],
        scratch_shapes=[pltpu.VMEM((H_C, DK_C, DV_C), F32)],
        compiler_params=_cparams(("arbitrary", "arbitrary")),
        name="gla_prompt",
    )(hm, hm, hm, hm, hm, wg2p, b_gate.reshape(1, -1), b_r.reshape(1, -1), gn.reshape(1, -1))


def _gla_sample_kernel(q_ref, k_ref, v_ref, r_ref, gt_ref, s_ref, wg2_ref, bg_ref, br_ref, gn_ref,
                       o_ref, so_ref):
    hi = lax.Precision.HIGHEST
    z = jnp.dot(jnp.broadcast_to(gt_ref[...], (8, LANE)), wg2_ref[...], preferred_element_type=F32,
                precision=hi)[0:1] + bg_ref[...]
    g_all = _log_sigmoid(z) * (1.0 / GATE_TAU)
    ri = lax.broadcasted_iota(jnp.int32, (DK_C, DK_C), 0)
    ci = lax.broadcasted_iota(jnp.int32, (DK_C, DK_C), 1)
    eye = ri == ci
    for h in range(H_C):
        eg = jnp.exp(g_all[:, h * DK_C:(h + 1) * DK_C])
        q = q_ref[h:h + 1, :] * (DK_C ** -0.5)
        k = k_ref[h:h + 1, :]
        v = v_ref[h:h + 1, :]
        state = s_ref[h]
        lhs = jnp.concatenate([jnp.where(eye, jnp.broadcast_to(eg, (DK_C, DK_C)), 0.0),
                               jnp.where(eye, jnp.broadcast_to(k, (DK_C, DK_C)), 0.0)], axis=1)
        rhs = jnp.concatenate([state, jnp.broadcast_to(v, (DK_C, DV_C))], axis=0)
        so_ref[h] = jnp.dot(lhs, rhs, preferred_element_type=F32, precision=hi)
        qd = q * eg
        o = (jnp.dot(jnp.broadcast_to(qd, (8, DK_C)).astype(BF16), state.astype(BF16),
                     preferred_element_type=F32)[0:1]
             + jnp.sum(q * k, axis=-1, keepdims=True) * v)
        r = _silu(r_ref[h:h + 1, :] + br_ref[:, h * DV_C:(h + 1) * DV_C])
        on = o * lax.rsqrt(jnp.mean(o * o, axis=-1, keepdims=True) + RMS_EPS) * gn_ref[...]
        o_ref[h:h + 1, :] = on * r


def _gla_sample(q_s, k_s, v_s, r_s, gt_s, state, wg2p, b_gate, b_r, gn):
    db = q_s.shape[0]
    b3 = lambda shape: pl.BlockSpec(shape, lambda b: (b, 0, 0))
    b4 = lambda shape: pl.BlockSpec(shape, lambda b: (b, 0, 0, 0))
    const = lambda shape: pl.BlockSpec(shape, lambda b: (0, 0))
    return pl.pallas_call(
        _gla_sample_kernel,
        grid=(db,),
        in_specs=[b3((None, H_C, DK_C)), b3((None, H_C, DK_C)), b3((None, H_C, DV_C)), b3((None, H_C, DV_C)),
                  b3((None, 1, LANE)), b4((None, H_C, DK_C, DV_C)),
                  const((LANE, H_C * DK_C)), const((1, H_C * DK_C)), const((1, H_C * DV_C)), const((1, DV_C))],
        out_specs=[b3((None, H_C, DV_C)), b4((None, H_C, DK_C, DV_C))],
        out_shape=[jax.ShapeDtypeStruct((db, H_C, DV_C), F32),
                   jax.ShapeDtypeStruct((db, H_C, DK_C, DV_C), F32)],
        compiler_params=_cparams(("arbitrary",)),
        name="gla_sample",
    )(q_s, k_s, v_s, r_s, gt_s, state, wg2p, b_gate.reshape(1, -1), b_r.reshape(1, -1), gn.reshape(1, -1))


def _swa_head_perm():
    perm = np.zeros((H_B * HD_B,), np.int32)
    for p in range(G_B):
        for kv in range(KV_B):
            for d in range(HD_B):
                perm[p * LANE + kv * HD_B + d] = (kv * G_B + p) * HD_B + d
    return perm


def _sample_rows(hm, np_rows, db):
    return jnp.transpose(hm[:, np_rows:np_rows + db, :], (1, 0, 2)).astype(F32)


def kernel(x_prompt, x_sample, cache_k_a, cache_v_a, state_swa_k, state_swa_v, state_gla, page_table,
           w_qkv_a, lam_a, subln_a, w_o_a, w_qkv_b, b_qkv_b, sinks_b, w_o_b, b_o_b,
           w_in_c, w_gate2_c, b_gate_c, b_r_c, gn_c, w_o_c, ln1_g, ln1_b, ln2_g, ln2_b,
           w_gu_d, w_down_d, w_router, w_gu_e, w_down_e):
    batch, t_len, _ = x_prompt.shape
    db = x_sample.shape[0]
    assert x_sample.shape[1] == 1 and db <= SAMPLE_PAD
    n_p = batch * t_len
    nt = n_p + SAMPLE_PAD
    n_real = n_p + db
    assert nt % TOK_TILE == 0

    y = jnp.concatenate([x_prompt.reshape(n_p, D_MODEL), x_sample.reshape(db, D_MODEL),
                         jnp.zeros((SAMPLE_PAD - db, D_MODEL), F32)], axis=0)
    yb = y.astype(BF16)
    cache_k = cache_k_a.reshape(cache_k_a.shape[:2] + (PAGE_SIZE * KV_A, LANE))
    cache_v = cache_v_a.reshape(cache_v_a.shape[:2] + (PAGE_SIZE * KV_A, LANE))
    w_gu_e2 = w_gu_e.reshape((-1,) + w_gu_e.shape[2:])
    w_down_e2 = w_down_e.reshape((-1,) + w_down_e.shape[2:])
    zero_bias = jnp.zeros((D_MODEL,), F32)
    perm = _swa_head_perm()
    ffn_tile = FFN_TILE if nt % FFN_TILE == 0 else TOK_TILE
    n_ffn_tiles = nt // ffn_tile

    ka_p, va_p, ka_s, va_s = [], [], [], []
    kb_p, vb_p, kb_s, vb_s = [], [], [], []
    gc_p, gc_s = [], []
    for i in range(DEPTH):
        j = i // 3
        if i % 3 == 0:
            nq = H_A * LANE
            (q_hm,) = _proj(yb, w_qkv_a, jnp.zeros((nq,), F32), layer=j, col0=0, ncols=nq, cw=512,
                            hm_dtype=BF16, emit_tok=None)
            kv_hm, kv_tok = _proj(yb, w_qkv_a, jnp.zeros((nq,), F32), layer=j, col0=nq, ncols=nq, cw=nq,
                                  hm_dtype=BF16, emit_tok="3d")
            ka_p.append(kv_tok[:n_p, :KV_A].reshape(batch, t_len, KV_A, LANE))
            va_p.append(kv_tok[:n_p, KV_A:].reshape(batch, t_len, KV_A, LANE))
            k_new = kv_tok[n_p:n_real, :KV_A]
            v_new = kv_tok[n_p:n_real, KV_A:]
            ka_s.append(k_new.reshape(db, 1, KV_A, LANE))
            va_s.append(v_new.reshape(db, 1, KV_A, LANE))
            lam_full, lam_init = _diff_lambda(lam_a[j], i)
            attn_p = _diffattn_prompt(q_hm, kv_hm, lam_full, lam_init, subln_a[j], batch, t_len)
            attn_s = _diffattn_sample(_sample_rows(q_hm, n_p, db), k_new, v_new, cache_k, cache_v, j,
                                      page_table, lam_full, lam_init, subln_a[j]).reshape(db, H_A * LANE)
            w_o, lay_o, b_o = w_o_a, j, zero_bias
        elif i % 3 == 1:
            nq = H_B * HD_B
            w_q = w_qkv_b[j][:, :nq][:, perm][None]
            b_q = b_qkv_b[j][:nq][perm]
            w_kv = w_qkv_b[j][:, nq:][None]
            b_kv = b_qkv_b[j][nq:]
            (q_hm,) = _proj(yb, w_q, b_q, layer=0, col0=0, ncols=nq, cw=512, hm_dtype=BF16, emit_tok=None)
            kv_hm, kv_tok = _proj(yb, w_kv, b_kv, layer=0, col0=0, ncols=2 * LANE, cw=2 * LANE,
                                  hm_dtype=BF16, emit_tok="2d")
            k_tok = kv_tok[:, :LANE]
            v_tok = kv_tok[:, LANE:]
            last_w = lambda a: a[:n_p].reshape(batch, t_len, LANE)[:, t_len - WINDOW:].reshape(
                batch, WINDOW, KV_B, HD_B)
            kb_p.append(last_w(k_tok))
            vb_p.append(last_w(v_tok))
            k_new = k_tok[n_p:n_real].reshape(db, 1, LANE)
            v_new = v_tok[n_p:n_real].reshape(db, 1, LANE)
            k_buf = state_swa_k[j].reshape(db, WINDOW, LANE)
            v_buf = state_swa_v[j].reshape(db, WINDOW, LANE)
            kb_s.append(jnp.concatenate([k_buf[:, 1:], k_new], axis=1).reshape(db, WINDOW, KV_B, HD_B))
            vb_s.append(jnp.concatenate([v_buf[:, 1:], v_new], axis=1).reshape(db, WINDOW, KV_B, HD_B))
            attn_p = _swa_prompt(q_hm, kv_hm, sinks_b[j], batch, t_len)
            attn_s = _swa_sample(_sample_rows(q_hm, n_p, db), k_buf, v_buf, k_new, v_new,
                                 sinks_b[j]).reshape(db, G_B * LANE)
            w_o, lay_o, b_o = w_o_b[j][perm][None], 0, b_o_b[j]
        else:
            n_in = w_in_c.shape[2]
            n_main = 2 * H_C * DK_C + 2 * H_C * DV_C
            w_in = jnp.zeros((1, D_MODEL, n_main + LANE), F32).at[0, :, :n_in].set(w_in_c[j])
            (hm,) = _proj(yb, w_in, jnp.zeros((n_main + LANE,), F32), layer=0, col0=0,
                          ncols=n_main + LANE, cw=5 * LANE, hm_dtype=F32, emit_tok=None)
            wg2p = jnp.zeros((LANE, H_C * DK_C), F32).at[:GATE_RANK].set(w_gate2_c[j])
            attn_p, st_p = _gla_prompt(hm, wg2p, b_gate_c[j], b_r_c[j], gn_c[j], batch, t_len)
            gc_p.append(st_p)
            hs = _sample_rows(hm, n_p, db)
            q_s = hs[:, 0:H_C]
            k_s = hs[:, H_C:2 * H_C]
            v_s = hs[:, 2 * H_C:4 * H_C].reshape(db, H_C, DV_C)
            r_s = hs[:, 4 * H_C:6 * H_C].reshape(db, H_C, DV_C)
            gt_s = hs[:, 6 * H_C:6 * H_C + 1]
            attn_s, st_s = _gla_sample(q_s, k_s, v_s, r_s, gt_s, state_gla[j].astype(F32), wg2p,
                                       b_gate_c[j], b_r_c[j], gn_c[j])
            gc_s.append(st_s)
            attn_s = attn_s.reshape(db, H_C * DV_C)
            w_o, lay_o, b_o = w_o_c, j, zero_bias

        f = i // 2
        if i % 2 == 0:
            y1, y1b = _outproj_ln(attn_p, attn_s, w_o, lay_o, b_o, y, ln1_g[i], ln1_b[i])
            y, yb = _ffn(y1b, w_gu_d, w_down_d, jnp.full((n_ffn_tiles,), f, jnp.int32),
                         jnp.ones((n_ffn_tiles,), jnp.int32), tm=ffn_tile,
                         dense_args=(y1, ln2_g[i], ln2_b[i]))
        else:
            y1, _, route = _outproj_ln(attn_p, attn_s, w_o, lay_o, b_o, y, ln1_g[i], ln1_b[i], w_router[f])
            y, yb = _moe(y1, route, n_real, w_gu_e2, w_down_e2, f * N_EXPERTS, ln2_g[i], ln2_b[i])

    yp = y[:n_p].reshape(batch, t_len, D_MODEL)
    ys = y[n_p:n_real].reshape(db, 1, D_MODEL)
    return (yp, ys, jnp.stack(ka_p), jnp.stack(va_p), jnp.stack(ka_s), jnp.stack(va_s),
            jnp.stack(kb_p), jnp.stack(vb_p), jnp.stack(kb_s), jnp.stack(vb_s),
            jnp.stack(gc_p), jnp.stack(gc_s))
```

```python
import functools
import math

import jax
import jax.numpy as jnp
import numpy as np
from jax import lax
from jax.experimental import pallas as pl
from jax.experimental.pallas import tpu as pltpu

F32 = jnp.float32
BF16 = jnp.bfloat16

D_MODEL = 1024
DEPTH = 4
PAGE_SIZE = 128
H_A, KV_A, G_A, HD_A = 8, 4, 2, 64
H_B, KV_B, G_B, HD_B = 16, 2, 8, 64
WINDOW = 128
H_C, DK_C, DV_C = 4, 128, 256
GATE_RANK = 16
GATE_TAU = 16.0
D_FF = 2816
N_EXPERTS = 8
LN_EPS = 1e-5
RMS_EPS = 1e-6
ALPHA = (2 * DEPTH) ** 0.25

LANE = 128
VMEM_LIMIT = 56 * 1024 * 1024
NEG_INF = float("-inf")

SAMPLE_PAD = 512
TOK_TILE = 768
FF_TILE = 256
MOE_TILE = 1024
FFN_TILE = 1056
Q_TILE = 256
K_TILE = 512
GLA_SUB = 64
GLA_TILE = 256
PAGES_PER_STEP = 16


def _cparams(sem):
    return pltpu.CompilerParams(dimension_semantics=sem, vmem_limit_bytes=VMEM_LIMIT)


def _layer_norm_rows(x, g, b):
    mu = jnp.mean(x, axis=-1, keepdims=True)
    xc = x - mu
    var = jnp.mean(xc * xc, axis=-1, keepdims=True)
    return xc * lax.rsqrt(var + LN_EPS) * g + b


def _proj_kernel(x_ref, w_ref, b_ref, *rest, emit_tok):
    hm_ref, wb_ref = rest[0], rest[-1]
    tok_refs = rest[1:-1]

    @pl.when(pl.program_id(1) == 0)
    def _():
        wb_ref[...] = w_ref[...].astype(BF16)

    y = jnp.dot(x_ref[...], wb_ref[...], preferred_element_type=F32) + b_ref[...]
    for c in range(hm_ref.shape[0]):
        hm_ref[c] = y[:, c * LANE:(c + 1) * LANE].astype(hm_ref.dtype)
        if emit_tok == "3d":
            per = tok_refs[0].shape[1]
            tok_refs[c // per][:, c % per, :] = y[:, c * LANE:(c + 1) * LANE]
    if emit_tok == "2d":
        tok_refs[0][...] = y


def _proj(xb, w, b, *, layer, col0, ncols, cw, hm_dtype, emit_tok, tm=TOK_TILE):
    nt = xb.shape[0]
    assert ncols % cw == 0 and col0 % cw == 0 and cw % LANE == 0 and nt % tm == 0
    nb = cw // LANE
    j0 = col0 // cw
    out_shape = [jax.ShapeDtypeStruct((ncols // LANE, nt, LANE), hm_dtype)]
    out_specs = [pl.BlockSpec((nb, tm, LANE), lambda j, i: (j, i, 0))]
    if emit_tok == "3d":
        assert cw == ncols and nb % KV_A == 0
        for _ in range(nb // KV_A):
            out_shape.append(jax.ShapeDtypeStruct((nt, KV_A, LANE), F32))
            out_specs.append(pl.BlockSpec((tm, KV_A, LANE), lambda j, i: (i, 0, 0)))
    elif emit_tok == "2d":
        out_shape.append(jax.ShapeDtypeStruct((nt, ncols), F32))
        out_specs.append(pl.BlockSpec((tm, cw), lambda j, i: (i, j)))
    return pl.pallas_call(
        functools.partial(_proj_kernel, emit_tok=emit_tok),
        grid=(ncols // cw, nt // tm),
        in_specs=[pl.BlockSpec((tm, D_MODEL), lambda j, i: (i, 0)),
                  pl.BlockSpec((None, D_MODEL, cw), lambda j, i: (layer, 0, j0 + j)),
                  pl.BlockSpec((1, cw), lambda j, i: (0, j))],
        out_specs=out_specs,
        out_shape=out_shape,
        scratch_shapes=[pltpu.VMEM((D_MODEL, cw), BF16)],
        compiler_params=_cparams(("arbitrary", "arbitrary")),
        name="proj",
    )(xb, w, b.reshape(1, ncols))


def _top2_route(logits):
    lane = lax.broadcasted_iota(jnp.int32, logits.shape, 1)
    l1 = jnp.where(lane < N_EXPERTS, logits, NEG_INF)
    m1 = jnp.max(l1, axis=-1, keepdims=True)
    i1 = jnp.min(jnp.where(l1 == m1, lane, LANE), axis=-1, keepdims=True)
    l2 = jnp.where(lane == i1, NEG_INF, l1)
    m2 = jnp.max(l2, axis=-1, keepdims=True)
    i2 = jnp.min(jnp.where(l2 == m2, lane, LANE), axis=-1, keepdims=True)
    e = jnp.exp(m2 - m1)
    g1 = 1.0 / (1.0 + e)
    g2 = e / (1.0 + e)
    return jnp.where(lane == 0, g1,
                     jnp.where(lane == 1, g2,
                               jnp.where(lane == 2, i1.astype(F32),
                                         jnp.where(lane == 3, i2.astype(F32), 0.0))))


def _outproj_kernel(ap_ref, as_ref, w_ref, b_ref, res_ref, g_ref, be_ref, *rest, route):
    if route:
        wr_ref, y_ref, yb_ref, rt_ref, wb_ref = rest
    else:
        y_ref, yb_ref, wb_ref = rest

    @pl.when(pl.program_id(0) == 0)
    def _():
        wb_ref[...] = w_ref[...].astype(BF16)

    a = jnp.where(pl.program_id(0) == pl.num_programs(0) - 1, as_ref[...], ap_ref[...])
    m = jnp.dot(a, wb_ref[...], preferred_element_type=F32) + b_ref[...]
    y = _layer_norm_rows(ALPHA * res_ref[...] + m, g_ref[...], be_ref[...])
    y_ref[...] = y
    yb_ref[...] = y.astype(BF16)
    if route:
        logits = jnp.dot(y, wr_ref[...], preferred_element_type=F32, precision=lax.Precision.HIGHEST)
        rt_ref[...] = _top2_route(logits)


def _outproj_ln(a_p, a_s, w, layer, b, res, g, be, w_router=None):
    tm = SAMPLE_PAD
    n_p, k = a_p.shape
    nt = res.shape[0]
    assert n_p % tm == 0 and nt == n_p + tm
    a_s = jnp.concatenate([a_s.astype(BF16), jnp.zeros((tm - a_s.shape[0], k), BF16)], axis=0)
    n_ptiles = n_p // tm
    route = w_router is not None
    row = lambda i: (i, 0)
    const = lambda i: (0, 0)
    in_specs = [pl.BlockSpec((tm, k), lambda i: (jnp.minimum(i, n_ptiles - 1), 0)),
                pl.BlockSpec((tm, k), const), pl.BlockSpec((None, k, D_MODEL), lambda i: (layer, 0, 0)),
                pl.BlockSpec((1, D_MODEL), const), pl.BlockSpec((tm, D_MODEL), row),
                pl.BlockSpec((1, D_MODEL), const), pl.BlockSpec((1, D_MODEL), const)]
    args = [a_p, a_s, w, b.reshape(1, D_MODEL), res, g.reshape(1, D_MODEL), be.reshape(1, D_MODEL)]
    out_shape = [jax.ShapeDtypeStruct((nt, D_MODEL), F32), jax.ShapeDtypeStruct((nt, D_MODEL), BF16)]
    out_specs = [pl.BlockSpec((tm, D_MODEL), row), pl.BlockSpec((tm, D_MODEL), row)]
    if route:
        wr = jnp.zeros((D_MODEL, LANE), F32).at[:, :N_EXPERTS].set(w_router)
        in_specs.append(pl.BlockSpec((D_MODEL, LANE), const))
        args.append(wr)
        out_shape.append(jax.ShapeDtypeStruct((nt, LANE), F32))
        out_specs.append(pl.BlockSpec((tm, LANE), row))
    return pl.pallas_call(
        functools.partial(_outproj_kernel, route=route),
        grid=(nt // tm,),
        in_specs=in_specs, out_specs=out_specs, out_shape=out_shape,
        scratch_shapes=[pltpu.VMEM((k, D_MODEL), BF16)],
        compiler_params=_cparams(("arbitrary",)),
        name="outproj_ln",
    )(*args)


def _ffn_kernel(te_ref, tv_ref, x_ref, wg_ref, wu_ref, wd_ref, *rest, dense):
    if dense:
        res_ref, g_ref, be_ref, y_ref, yb_ref, acc_ref = rest
    else:
        y_ref, acc_ref, xb_ref = rest
    i = pl.program_id(0)
    j = pl.program_id(1)
    nj = pl.num_programs(1)

    @pl.when(tv_ref[i] > 0)
    def _():
        if dense:
            x = x_ref[...]
        else:
            @pl.when(j == 0)
            def _():
                xb_ref[...] = x_ref[...].astype(BF16)
            x = xb_ref[...]
        gate = jnp.dot(x, wg_ref[...].astype(BF16), preferred_element_type=F32)
        up = jnp.dot(x, wu_ref[...].astype(BF16), preferred_element_type=F32)
        h = (gate * (1.0 / (1.0 + jnp.exp(-gate))) * up).astype(BF16)
        part = jnp.dot(h, wd_ref[...].astype(BF16), preferred_element_type=F32)

        @pl.when(j == 0)
        def _():
            acc_ref[...] = part

        @pl.when(j > 0)
        def _():
            acc_ref[...] += part

        @pl.when(j == nj - 1)
        def _():
            if dense:
                y = _layer_norm_rows(ALPHA * res_ref[...] + acc_ref[...], g_ref[...], be_ref[...])
                y_ref[...] = y
                yb_ref[...] = y.astype(BF16)
            else:
                y_ref[...] = acc_ref[...]

    if not dense:
        @pl.when(jnp.logical_and(tv_ref[i] == 0, j == nj - 1))
        def _():
            y_ref[...] = jnp.zeros_like(y_ref)


def _ffn(xb, w_gu, w_down, tile_expert, tile_valid, *, tm, dense_args=None, tf=FF_TILE):
    m = xb.shape[0]
    assert m % tm == 0 and D_FF % tf == 0
    nf = D_FF // tf
    dense = dense_args is not None

    def jeff(i, j, tv):
        return jnp.where(tv[i] > 0, j, nf - 1)

    in_specs = [
        pl.BlockSpec((tm, D_MODEL), lambda i, j, te, tv: (i, 0)),
        pl.BlockSpec((None, D_MODEL, tf), lambda i, j, te, tv: (te[i], 0, jeff(i, j, tv))),
        pl.BlockSpec((None, D_MODEL, tf), lambda i, j, te, tv: (te[i], 0, nf + jeff(i, j, tv))),
        pl.BlockSpec((None, tf, D_MODEL), lambda i, j, te, tv: (te[i], jeff(i, j, tv), 0)),
    ]
    args = [xb, w_gu, w_gu, w_down]
    row = lambda i, j, te, tv: (i, 0)
    const = lambda i, j, te, tv: (0, 0)
    if dense:
        res, g, be = dense_args
        in_specs += [pl.BlockSpec((tm, D_MODEL), row), pl.BlockSpec((1, D_MODEL), const),
                     pl.BlockSpec((1, D_MODEL), const)]
        args += [res, g.reshape(1, D_MODEL), be.reshape(1, D_MODEL)]
        out_shape = [jax.ShapeDtypeStruct((m, D_MODEL), F32), jax.ShapeDtypeStruct((m, D_MODEL), BF16)]
        out_specs = [pl.BlockSpec((tm, D_MODEL), row), pl.BlockSpec((tm, D_MODEL), row)]
    else:
        out_shape = jax.ShapeDtypeStruct((m, D_MODEL), F32)
        out_specs = pl.BlockSpec((tm, D_MODEL), row)
    scratch = [pltpu.VMEM((tm, D_MODEL), F32)] + ([] if dense else [pltpu.VMEM((tm, D_MODEL), BF16)])
    return pl.pallas_call(
        functools.partial(_ffn_kernel, dense=dense),
        grid_spec=pltpu.PrefetchScalarGridSpec(
            num_scalar_prefetch=2, grid=(m // tm, nf),
            in_specs=in_specs, out_specs=out_specs, scratch_shapes=scratch),
        out_shape=out_shape,
        compiler_params=_cparams(("arbitrary", "arbitrary")),
        name="ffn_dense" if dense else "ffn_moe",
    )(tile_expert, tile_valid, *args)


def _combine_kernel(ya_ref, yb_ref, rt_ref, res_ref, g_ref, be_ref, y_ref, ybf_ref):
    rt = rt_ref[...]
    f = rt[:, 0:1] * ya_ref[...] + rt[:, 1:2] * yb_ref[...]
    y = _layer_norm_rows(ALPHA * res_ref[...] + f, g_ref[...], be_ref[...])
    y_ref[...] = y
    ybf_ref[...] = y.astype(BF16)


def _combine_ln(ya, yb, route, res, g, be, *, tm=TOK_TILE):
    nt = res.shape[0]
    row = lambda i: (i, 0)
    const = lambda i: (0, 0)
    return pl.pallas_call(
        _combine_kernel,
        grid=(nt // tm,),
        in_specs=[pl.BlockSpec((tm, D_MODEL), row), pl.BlockSpec((tm, D_MODEL), row),
                  pl.BlockSpec((tm, LANE), row), pl.BlockSpec((tm, D_MODEL), row),
                  pl.BlockSpec((1, D_MODEL), const), pl.BlockSpec((1, D_MODEL), const)],
        out_specs=[pl.BlockSpec((tm, D_MODEL), row), pl.BlockSpec((tm, D_MODEL), row)],
        out_shape=[jax.ShapeDtypeStruct((nt, D_MODEL), F32), jax.ShapeDtypeStruct((nt, D_MODEL), BF16)],
        compiler_params=_cparams(("arbitrary",)),
        name="moe_combine_ln",
    )(ya, yb, route, res, g.reshape(1, D_MODEL), be.reshape(1, D_MODEL))


def _moe(res, route, n_real, w_gu, w_down, e_off, g, be, *, tm=MOE_TILE):
    nt = res.shape[0]
    gates_idx = route[:n_real, 2:4].astype(jnp.int32)
    flat_e = gates_idx.reshape(-1)
    npair = flat_e.shape[0]
    onehot = (flat_e[:, None] == jnp.arange(N_EXPERTS)[None, :]).astype(jnp.int32)
    csum = jnp.cumsum(onehot, axis=0)
    counts = csum[-1]
    rank = jnp.take_along_axis(csum, flat_e[:, None], axis=1)[:, 0] - 1
    tiles_per = (counts + tm - 1) // tm
    tile_end = jnp.cumsum(tiles_per)
    tile_start = tile_end - tiles_per
    n_tiles = npair // tm + N_EXPERTS
    m_pad = n_tiles * tm
    dest = tile_start[flat_e] * tm + rank
    row_src = jnp.zeros((m_pad,), jnp.int32).at[dest].set(
        jnp.arange(npair, dtype=jnp.int32) // 2, unique_indices=True, mode="promise_in_bounds")
    t_ids = jnp.arange(n_tiles, dtype=jnp.int32)
    tile_valid = (t_ids < tile_end[-1]).astype(jnp.int32)
    tile_expert = jnp.sum((t_ids[:, None] >= tile_end[None, :]).astype(jnp.int32), axis=1)
    tile_expert = jnp.minimum(tile_expert, N_EXPERTS - 1)
    last_e = tile_expert[jnp.maximum(tile_end[-1] - 1, 0)]
    tile_expert = jnp.where(tile_valid > 0, tile_expert, last_e) + e_off
    x_sorted = res.at[row_src].get(mode="promise_in_bounds")
    y_sorted = _ffn(x_sorted, w_gu, w_down, tile_expert, tile_valid, tm=tm)
    dest2 = jnp.zeros((nt, 2), jnp.int32).at[:n_real].set(dest.reshape(n_real, 2))
    ya = y_sorted.at[dest2[:, 0]].get(mode="promise_in_bounds")
    ybb = y_sorted.at[dest2[:, 1]].get(mode="promise_in_bounds")
    return _combine_ln(ya, ybb, route, res, g, be)


def _diff_lambda(lam, layer_idx):
    lam_init = 0.8 - 0.6 * math.exp(-0.3 * layer_idx)
    lf = lam.astype(F32)
    lam_full = jnp.exp(jnp.sum(lf[0] * lf[1])) - jnp.exp(jnp.sum(lf[2] * lf[3])) + lam_init
    return lam_full, lam_init


def _alibi_slopes(n):
    return [2.0 ** (-8.0 * (i + 1) / n) for i in range(n)]


def _sub_rms(o, subln, out_scale):
    return o * lax.rsqrt(jnp.mean(o * o, axis=-1, keepdims=True) + RMS_EPS) * subln * out_scale


def _diffattn_prompt_kernel(q_ref, k_ref, v_ref, pos_ref, slope_ref, lam_ref, subln_ref, o_ref,
                            kaug_ref, vt_ref, qaug_ref, m_ref, l_ref, acc_ref, *, out_scale):
    qi = pl.program_id(2)
    tq = q_ref.shape[1]
    t_len = k_ref.shape[0]
    nrow = 2 * G_A * tq

    @pl.when(qi == 0)
    def _():
        kaug_ref[:, :LANE] = k_ref[...]
        kaug_ref[:, LANE:] = pos_ref[...]
        for jb in range(t_len // K_TILE):
            vt_ref[jb] = v_ref[jb * K_TILE:(jb + 1) * K_TILE, :].astype(F32).T.astype(BF16)

    lane = lax.broadcasted_iota(jnp.int32, (tq, LANE), 1)
    scale = HD_A ** -0.5
    for g in range(G_A):
        qg = q_ref[g] * scale
        qaug_ref[(2 * g) * tq:(2 * g + 1) * tq, :LANE] = jnp.where(lane < HD_A, qg, 0).astype(BF16)
        qaug_ref[(2 * g + 1) * tq:(2 * g + 2) * tq, :LANE] = jnp.where(lane >= HD_A, qg, 0).astype(BF16)
    qaug_ref[:, LANE:] = slope_ref[...]

    m_ref[...] = jnp.full(m_ref.shape, NEG_INF, F32)
    l_ref[...] = jnp.zeros(l_ref.shape, F32)
    acc_ref[...] = jnp.zeros(acc_ref.shape, F32)

    def block(j, masked):
        start = pl.multiple_of(j * K_TILE, K_TILE)
        kb = kaug_ref[pl.ds(start, K_TILE), :]
        s = lax.dot_general(kb, qaug_ref[...], (((1,), (1,)), ((), ())), preferred_element_type=F32)
        if masked:
            key = lax.broadcasted_iota(jnp.int32, (K_TILE, nrow), 0)
            col = lax.broadcasted_iota(jnp.int32, (K_TILE, nrow), 1)
            s = jnp.where(start + key <= qi * tq + (col & (tq - 1)), s, NEG_INF)
        m_old = m_ref[...]
        m_new = jnp.maximum(m_old, jnp.max(s, axis=0, keepdims=True))
        alpha = jnp.exp(m_old - m_new)
        p = jnp.exp(s - m_new)
        l_ref[...] = alpha * l_ref[...] + jnp.sum(p, axis=0, keepdims=True)
        acc_ref[...] = alpha * acc_ref[...] + jnp.dot(vt_ref[j], p.astype(BF16), preferred_element_type=F32)
        m_ref[...] = m_new

    n_full = (qi * tq) // K_TILE

    def body(j, carry):
        block(j, False)
        return carry

    lax.fori_loop(0, n_full, body, 0)
    block(n_full, True)

    lam = lam_ref[...]
    norm = acc_ref[...] / l_ref[...]
    for g in range(G_A):
        o = norm[:, (2 * g) * tq:(2 * g + 1) * tq].T - lam * norm[:, (2 * g + 1) * tq:(2 * g + 2) * tq].T
        o_ref[:, g * LANE:(g + 1) * LANE] = _sub_rms(o, subln_ref[...], out_scale).astype(o_ref.dtype)


def _alibi_aug_consts(t_len, tq):
    pos = np.arange(t_len)
    kcols = np.zeros((t_len, LANE), np.float32)
    kcols[:, 0] = (pos // 64) * 64
    kcols[:, 1] = pos % 64
    slopes = _alibi_slopes(H_A)
    qcols = np.zeros((KV_A, 2 * G_A * tq, LANE), np.float32)
    for kv in range(KV_A):
        for g in range(G_A):
            qcols[kv, 2 * g * tq:(2 * g + 2) * tq, 0:2] = slopes[kv * G_A + g]
    return jnp.asarray(kcols, BF16), jnp.asarray(qcols, BF16)


def _diffattn_prompt(q_hm, kv_hm, lam_full, lam_init, subln, batch, t_len):
    tq = Q_TILE
    nq = t_len // tq
    assert t_len % K_TILE == 0
    kcols, qcols = _alibi_aug_consts(t_len, tq)
    nrow = 2 * G_A * tq
    lam_row = jnp.full((1, LANE), lam_full, F32)
    return pl.pallas_call(
        functools.partial(_diffattn_prompt_kernel, out_scale=1.0 - lam_init),
        grid=(batch, KV_A, nq),
        in_specs=[pl.BlockSpec((G_A, tq, LANE), lambda b, kv, qi: (kv, b * nq + qi, 0)),
                  pl.BlockSpec((None, t_len, LANE), lambda b, kv, qi: (kv, b, 0)),
                  pl.BlockSpec((None, t_len, LANE), lambda b, kv, qi: (KV_A + kv, b, 0)),
                  pl.BlockSpec((t_len, LANE), lambda b, kv, qi: (0, 0)),
                  pl.BlockSpec((None, nrow, LANE), lambda b, kv, qi: (kv, 0, 0)),
                  pl.BlockSpec((1, LANE), lambda b, kv, qi: (0, 0)),
                  pl.BlockSpec((1, LANE), lambda b, kv, qi: (0, 0))],
        out_specs=pl.BlockSpec((tq, G_A * LANE), lambda b, kv, qi: (b * nq + qi, kv)),
        out_shape=jax.ShapeDtypeStruct((batch * t_len, H_A * LANE), BF16),
        scratch_shapes=[pltpu.VMEM((t_len, 2 * LANE), BF16), pltpu.VMEM((t_len // K_TILE, LANE, K_TILE), BF16),
                        pltpu.VMEM((nrow, 2 * LANE), BF16),
                        pltpu.VMEM((1, nrow), F32), pltpu.VMEM((1, nrow), F32), pltpu.VMEM((LANE, nrow), F32)],
        compiler_params=_cparams(("arbitrary", "arbitrary", "arbitrary")),
        name="diffattn_prompt",
    )(q_hm, kv_hm, kv_hm, kcols, qcols, lam_row, subln.reshape(1, LANE))


def _diffattn_sample_kernel(pt_ref, q_ref, kn_ref, vn_ref, lam_ref, subln_ref, slope_ref, *rest,
                            out_scale, past_len):
    pp = PAGES_PER_STEP
    k_refs = rest[:pp]
    v_refs = rest[pp:2 * pp]
    o_ref, m_ref, l_ref, acc_ref = rest[2 * pp:]
    c = pl.program_id(1)
    nc = pl.num_programs(1)
    nrow = KV_A * 8
    ncol = pp * PAGE_SIZE * KV_A

    @pl.when(c == 0)
    def _():
        m_ref[...] = jnp.full(m_ref.shape, NEG_INF, F32)
        l_ref[...] = jnp.zeros(l_ref.shape, F32)
        acc_ref[...] = jnp.zeros(acc_ref.shape, F32)

    kc = jnp.concatenate([r[...] for r in k_refs], axis=0).astype(BF16)
    vc = jnp.concatenate([r[...] for r in v_refs], axis=0).astype(BF16)
    s = lax.dot_general(q_ref[...].astype(BF16), kc, (((1,), (1,)), ((), ())), preferred_element_type=F32)
    row = lax.broadcasted_iota(jnp.int32, (nrow, ncol), 0)
    col = lax.broadcasted_iota(jnp.int32, (nrow, ncol), 1)
    pos = c * (pp * PAGE_SIZE) + (col >> 2)
    dist = past_len - pos.astype(F32)
    s = jnp.where((col & (KV_A - 1)) == (row >> 3), s - slope_ref[...] * dist, NEG_INF)
    m_old = m_ref[...]
    m_new = jnp.maximum(m_old, jnp.max(s, axis=-1, keepdims=True))
    alpha = jnp.exp(m_old - m_new)
    p = jnp.exp(s - m_new)
    l_ref[...] = alpha * l_ref[...] + jnp.sum(p, axis=-1, keepdims=True)
    acc_ref[...] = alpha * acc_ref[...] + jnp.dot(p.astype(BF16), vc, preferred_element_type=F32)
    m_ref[...] = m_new

    @pl.when(c == nc - 1)
    def _():
        lam = lam_ref[...]
        s_self = jnp.sum(q_ref[...] * kn_ref[...], axis=-1, keepdims=True)
        m_old = m_ref[...]
        m_fin = jnp.maximum(m_old, s_self)
        alpha = jnp.exp(m_old - m_fin)
        p_self = jnp.exp(s_self - m_fin)
        norm = (alpha * acc_ref[...] + p_self * vn_ref[...]) / (alpha * l_ref[...] + p_self)
        for kv in range(KV_A):
            for g in range(G_A):
                o = norm[kv * 8 + g:kv * 8 + g + 1] - lam * norm[kv * 8 + 4 + g:kv * 8 + 5 + g]
                o_ref[kv * G_A + g:kv * G_A + g + 1, :] = _sub_rms(o, subln_ref[...], out_scale)


def _diffattn_sample(q_s, k_new, v_new, cache_k, cache_v, layer_j, page_table, lam_full, lam_init, subln):
    db, n_pages = page_table.shape
    nrow = KV_A * 8
    qg = q_s.reshape(db, KV_A, G_A, LANE) * (HD_A ** -0.5)
    lo = jnp.arange(LANE) < HD_A
    zero = jnp.zeros((db, KV_A, 2, LANE), F32)
    q32 = jnp.concatenate([jnp.where(lo, qg, 0.0), zero, jnp.where(lo, 0.0, qg), zero], axis=2)
    q32 = q32.reshape(db, nrow, LANE)
    kn32 = jnp.repeat(k_new, 8, axis=1)
    vn32 = jnp.repeat(v_new, 8, axis=1)
    pp = PAGES_PER_STEP
    assert n_pages % pp == 0 and KV_A == 4
    past_len = n_pages * PAGE_SIZE
    slopes = np.zeros((nrow, 1), np.float32)
    sl = _alibi_slopes(H_A)
    for kv in range(KV_A):
        for g in range(G_A):
            slopes[kv * 8 + g, 0] = sl[kv * G_A + g]
            slopes[kv * 8 + 4 + g, 0] = sl[kv * G_A + g]
    lam_row = jnp.full((1, LANE), lam_full, F32)

    def page_spec(i):
        return pl.BlockSpec((None, None, PAGE_SIZE * KV_A, LANE),
                            lambda b, c, pt: (layer_j, pt[b, c * pp + i], 0, 0))

    vec = pl.BlockSpec((None, nrow, LANE), lambda b, c, pt: (b, 0, 0))
    const2 = lambda shape: pl.BlockSpec(shape, lambda b, c, pt: (0, 0))
    in_specs = ([vec, vec, vec, const2((1, LANE)), const2((1, LANE)), const2((nrow, 1))]
                + [page_spec(i) for i in range(pp)] * 2)
    return pl.pallas_call(
        functools.partial(_diffattn_sample_kernel, out_scale=1.0 - lam_init, past_len=float(past_len)),
        grid_spec=pltpu.PrefetchScalarGridSpec(
            num_scalar_prefetch=1, grid=(db, n_pages // pp),
            in_specs=in_specs,
            out_specs=pl.BlockSpec((None, H_A, LANE), lambda b, c, pt: (b, 0, 0)),
            scratch_shapes=[pltpu.VMEM((nrow, 1), F32), pltpu.VMEM((nrow, 1), F32),
                            pltpu.VMEM((nrow, LANE), F32)]),
        out_shape=jax.ShapeDtypeStruct((db, H_A, LANE), F32),
        compiler_params=_cparams(("arbitrary", "arbitrary")),
        name="diffattn_sample",
    )(page_table, q32, kn32, vn32, lam_row, subln.reshape(1, LANE), jnp.asarray(slopes),
      *([cache_k] * pp), *([cache_v] * pp))


def _swa_prompt_kernel(sink_ref, q_ref, kp_ref, kc_ref, vp_ref, vc_ref, o_ref):
    n = pl.program_id(1)
    w = WINDOW
    kband = jnp.concatenate([kp_ref[...], kc_ref[...]], axis=0)
    vband = jnp.concatenate([vp_ref[...], vc_ref[...]], axis=0)
    lane = lax.broadcasted_iota(jnp.int32, (w, LANE), 1)
    row = lax.broadcasted_iota(jnp.int32, (w, 2 * w), 0)
    col = lax.broadcasted_iota(jnp.int32, (w, 2 * w), 1)
    rel = (w + row) - col
    valid = jnp.logical_and(jnp.logical_and(rel >= 0, rel <= w), jnp.logical_or(col >= w, n > 0))
    relf = rel.astype(F32)
    slopes = _alibi_slopes(H_B)
    scale = HD_B ** -0.5
    o_kv = []
    for kv in range(KV_B):
        keep = (lane < HD_B) if kv == 0 else (lane >= HD_B)
        qs = jnp.concatenate([jnp.where(keep, q_ref[p] * scale, 0).astype(BF16) for p in range(G_B)], axis=0)
        s_all = lax.dot_general(qs, kband, (((1,), (1,)), ((), ())), preferred_element_type=F32)
        ps = []
        for p in range(G_B):
            h = kv * G_B + p
            s = jnp.where(valid, s_all[p * w:(p + 1) * w] - slopes[h] * relf, NEG_INF)
            sink = sink_ref[h]
            m = jnp.maximum(jnp.max(s, axis=-1, keepdims=True), sink)
            e = jnp.exp(s - m)
            denom = jnp.sum(e, axis=-1, keepdims=True) + jnp.exp(sink - m)
            ps.append((e / denom).astype(BF16))
        o_kv.append(jnp.dot(jnp.concatenate(ps, axis=0), vband, preferred_element_type=F32))
    for p in range(G_B):
        o = jnp.where(lane < HD_B, o_kv[0][p * w:(p + 1) * w], o_kv[1][p * w:(p + 1) * w])
        o_ref[:, p * LANE:(p + 1) * LANE] = o.astype(o_ref.dtype)


def _swa_prompt(q_hm, kv_hm, sinks, batch, t_len):
    w = WINDOW
    nb = t_len // w
    cur = lambda b, n, s: (b * nb + n, 0)
    prev = lambda b, n, s: (b * nb + jnp.maximum(n - 1, 0), 0)
    return pl.pallas_call(
        _swa_prompt_kernel,
        grid_spec=pltpu.PrefetchScalarGridSpec(
            num_scalar_prefetch=1, grid=(batch, nb),
            in_specs=[pl.BlockSpec((G_B, w, LANE), lambda b, n, s: (0,) + cur(b, n, s)),
                      pl.BlockSpec((None, w, LANE), lambda b, n, s: (0,) + prev(b, n, s)),
                      pl.BlockSpec((None, w, LANE), lambda b, n, s: (0,) + cur(b, n, s)),
                      pl.BlockSpec((None, w, LANE), lambda b, n, s: (1,) + prev(b, n, s)),
                      pl.BlockSpec((None, w, LANE), lambda b, n, s: (1,) + cur(b, n, s))],
            out_specs=pl.BlockSpec((w, G_B * LANE), cur)),
        out_shape=jax.ShapeDtypeStruct((batch * t_len, G_B * LANE), BF16),
        compiler_params=_cparams(("arbitrary", "arbitrary")),
        name="swa_prompt",
    )(sinks.astype(F32), q_hm, kv_hm, kv_hm, kv_hm, kv_hm)


def _swa_sample_kernel(q_ref, kb_ref, vb_ref, kn_ref, vn_ref, slope_ref, sink_ref, o_ref):
    w = WINDOW
    q = q_ref[...]
    s = lax.dot_general(q.astype(BF16), kb_ref[...].astype(BF16), (((1,), (1,)), ((), ())),
                        preferred_element_type=F32)
    dist = (w - lax.broadcasted_iota(jnp.int32, (H_B, w), 1)).astype(F32)
    s = s - slope_ref[...] * dist
    s_self = jnp.sum(q * kn_ref[...], axis=-1, keepdims=True)
    sink = sink_ref[...]
    m = jnp.maximum(jnp.maximum(jnp.max(s, axis=-1, keepdims=True), s_self), sink)
    e = jnp.exp(s - m)
    e_self = jnp.exp(s_self - m)
    denom = jnp.sum(e, axis=-1, keepdims=True) + e_self + jnp.exp(sink - m)
    o = (jnp.dot(e.astype(BF16), vb_ref[...].astype(BF16), preferred_element_type=F32)
         + e_self * vn_ref[...]) / denom
    lane = lax.broadcasted_iota(jnp.int32, (G_B, LANE), 1)
    o_ref[...] = jnp.where(lane < HD_B, o[:G_B], o[G_B:])


def _swa_sample(q_s, k_buf, v_buf, k_new, v_new, sinks):
    db = q_s.shape[0]
    lo = jnp.arange(LANE) < HD_B
    qs = q_s * (HD_B ** -0.5)
    q16 = jnp.concatenate([jnp.where(lo, qs, 0.0), jnp.where(lo, 0.0, qs)], axis=1)
    slopes = jnp.asarray(np.asarray(_alibi_slopes(H_B), np.float32).reshape(H_B, 1))
    per_b3 = lambda shape: pl.BlockSpec(shape, lambda b: (b, 0, 0))
    const = pl.BlockSpec((H_B, 1), lambda b: (0, 0))
    return pl.pallas_call(
        _swa_sample_kernel,
        grid=(db,),
        in_specs=[per_b3((None, H_B, LANE)), per_b3((None, WINDOW, LANE)), per_b3((None, WINDOW, LANE)),
                  per_b3((None, 1, LANE)), per_b3((None, 1, LANE)), const, const],
        out_specs=per_b3((None, G_B, LANE)),
        out_shape=jax.ShapeDtypeStruct((db, G_B, LANE), F32),
        compiler_params=_cparams(("arbitrary",)),
        name="swa_sample",
    )(q16, k_buf, v_buf, k_new, v_new, slopes, sinks.astype(F32).reshape(H_B, 1))


def _log_sigmoid(z):
    return jnp.minimum(z, 0.0) - jnp.log(1.0 + jnp.exp(-jnp.abs(z)))


def _silu(x):
    return x * (1.0 / (1.0 + jnp.exp(-x)))


def _gla_prompt_kernel(q_ref, k_ref, v_ref, r_ref, gt_ref, wg2_ref, bg_ref, br_ref, gn_ref,
                       o_ref, st_ref, s_ref):
    ct = pl.program_id(1)
    c = GLA_SUB
    rows_per_step = gt_ref.shape[0]

    @pl.when(ct == 0)
    def _():
        s_ref[...] = jnp.zeros(s_ref.shape, F32)

    z = jnp.dot(gt_ref[...], wg2_ref[...], preferred_element_type=F32,
                precision=lax.Precision.HIGHEST) + bg_ref[...]
    g_all = _log_sigmoid(z) * (1.0 / GATE_TAU)
    ri = lax.broadcasted_iota(jnp.int32, (c, c), 0)
    ci = lax.broadcasted_iota(jnp.int32, (c, c), 1)
    causal = ri >= ci
    tri = causal.astype(F32)
    for sub in range(rows_per_step // c):
        r0 = sub * c
        for h in range(H_C):
            g = g_all[r0:r0 + c, h * DK_C:(h + 1) * DK_C]
            bc = jnp.dot(tri, g, preferred_element_type=F32, precision=lax.Precision.HIGHEST)
            q = q_ref[h, r0:r0 + c, :] * (DK_C ** -0.5)
            k = k_ref[h, r0:r0 + c, :]
            v = jnp.concatenate([v_ref[2 * h, r0:r0 + c, :], v_ref[2 * h + 1, r0:r0 + c, :]], axis=1)
            vb = v.astype(BF16)
            q_dec = (q * jnp.exp(bc)).astype(BF16)
            k_dec = (k * jnp.exp(-bc)).astype(BF16)
            att = lax.dot_general(q_dec, k_dec, (((1,), (1,)), ((), ())), preferred_element_type=F32)
            att = jnp.where(causal, att, 0.0).astype(BF16)
            state = s_ref[h]
            o = (jnp.dot(q_dec, state.astype(BF16), preferred_element_type=F32)
                 + jnp.dot(att, vb, preferred_element_type=F32))
            b_last = bc[c - 1:c, :]
            k_tail = (k * jnp.exp(b_last - bc)).astype(BF16)
            decay_col = jnp.transpose(jnp.broadcast_to(jnp.exp(b_last), (8, DK_C)))[:, 0:1]
            s_ref[h] = decay_col * state + lax.dot_general(
                k_tail, vb, (((0,), (0,)), ((), ())), preferred_element_type=F32)
            r = jnp.concatenate([r_ref[2 * h, r0:r0 + c, :], r_ref[2 * h + 1, r0:r0 + c, :]], axis=1)
            r = _silu(r + br_ref[:, h * DV_C:(h + 1) * DV_C])
            on = o * lax.rsqrt(jnp.mean(o * o, axis=-1, keepdims=True) + RMS_EPS) * gn_ref[...]
            o_ref[r0:r0 + c, h * DV_C:(h + 1) * DV_C] = (on * r).astype(o_ref.dtype)

    @pl.when(ct == pl.num_programs(1) - 1)
    def _():
        st_ref[...] = s_ref[...]


def _gla_prompt(hm, wg2p, b_gate, b_r, gn, batch, t_len):
    rt = GLA_TILE
    nc = t_len // rt
    rowblk = lambda b, ct: b * nc + ct
    const = lambda b, ct: (0, 0)
    return pl.pallas_call(
        _gla_prompt_kernel,
        grid=(batch, nc),
        in_specs=[pl.BlockSpec((H_C, rt, LANE), lambda b, ct: (0, rowblk(b, ct), 0)),
                  pl.BlockSpec((H_C, rt, LANE), lambda b, ct: (1, rowblk(b, ct), 0)),
                  pl.BlockSpec((2 * H_C, rt, LANE), lambda b, ct: (1, rowblk(b, ct), 0)),
                  pl.BlockSpec((2 * H_C, rt, LANE), lambda b, ct: (2, rowblk(b, ct), 0)),
                  pl.BlockSpec((None, rt, LANE), lambda b, ct: (6 * H_C, rowblk(b, ct), 0)),
                  pl.BlockSpec((LANE, H_C * DK_C), const), pl.BlockSpec((1, H_C * DK_C), const),
                  pl.BlockSpec((1, H_C * DV_C), const), pl.BlockSpec((1, DV_C), const)],
        out_specs=[pl.BlockSpec((rt, H_C * DV_C), lambda b, ct: (rowblk(b, ct), 0)),
                   pl.BlockSpec((None, H_C, DK_C, DV_C), lambda b, ct: (b, 0, 0, 0))],
        out_shape=[jax.ShapeDtypeStruct((batch * t_len, H_C * DV_C), BF16),
                   jax.ShapeDtypeStruct((batch, H_C, DK_C, DV_C), F32)],
        scratch_shapes=[pltpu.VMEM((H_C, DK_C, DV_C), F32)],
        compiler_params=_cparams(("arbitrary", "arbitrary")),
        name="gla_prompt",
    )(hm, hm, hm, hm, hm, wg2p, b_gate.reshape(1, -1), b_r.reshape(1, -1), gn.reshape(1, -1))


def _gla_sample_kernel(q_ref, k_ref, v_ref, r_ref, gt_ref, s_ref, wg2_ref, bg_ref, br_ref, gn_ref,
                       o_ref, so_ref):
    hi = lax.Precision.HIGHEST
    z = jnp.dot(jnp.broadcast_to(gt_ref[...], (8, LANE)), wg2_ref[...], preferred_element_type=F32,
                precision=hi)[0:1] + bg_ref[...]
    g_all = _log_sigmoid(z) * (1.0 / GATE_TAU)
    ri = lax.broadcasted_iota(jnp.int32, (DK_C, DK_C), 0)
    ci = lax.broadcasted_iota(jnp.int32, (DK_C, DK_C), 1)
    eye = ri == ci
    for h in range(H_C):
        eg = jnp.exp(g_all[:, h * DK_C:(h + 1) * DK_C])
        q = q_ref[h:h + 1, :] * (DK_C ** -0.5)
        k = k_ref[h:h + 1, :]
        v = v_ref[h:h + 1, :]
        state = s_ref[h]
        lhs = jnp.concatenate([jnp.where(eye, jnp.broadcast_to(eg, (DK_C, DK_C)), 0.0),
                               jnp.where(eye, jnp.broadcast_to(k, (DK_C, DK_C)), 0.0)], axis=1)
        rhs = jnp.concatenate([state, jnp.broadcast_to(v, (DK_C, DV_C))], axis=0)
        so_ref[h] = jnp.dot(lhs, rhs, preferred_element_type=F32, precision=hi)
        qd = q * eg
        o = (jnp.dot(jnp.broadcast_to(qd, (8, DK_C)).astype(BF16), state.astype(BF16),
                     preferred_element_type=F32)[0:1]
             + jnp.sum(q * k, axis=-1, keepdims=True) * v)
        r = _silu(r_ref[h:h + 1, :] + br_ref[:, h * DV_C:(h + 1) * DV_C])
        on = o * lax.rsqrt(jnp.mean(o * o, axis=-1, keepdims=True) + RMS_EPS) * gn_ref[...]
        o_ref[h:h + 1, :] = on * r


def _gla_sample(q_s, k_s, v_s, r_s, gt_s, state, wg2p, b_gate, b_r, gn):
    db = q_s.shape[0]
    b3 = lambda shape: pl.BlockSpec(shape, lambda b: (b, 0, 0))
    b4 = lambda shape: pl.BlockSpec(shape, lambda b: (b, 0, 0, 0))
    const = lambda shape: pl.BlockSpec(shape, lambda b: (0, 0))
    return pl.pallas_call(
        _gla_sample_kernel,
        grid=(db,),
        in_specs=[b3((None, H_C, DK_C)), b3((None, H_C, DK_C)), b3((None, H_C, DV_C)), b3((None, H_C, DV_C)),
                  b3((None, 1, LANE)), b4((None, H_C, DK_C, DV_C)),
                  const((LANE, H_C * DK_C)), const((1, H_C * DK_C)), const((1, H_C * DV_C)), const((1, DV_C))],
        out_specs=[b3((None, H_C, DV_C)), b4((None, H_C, DK_C, DV_C))],
        out_shape=[jax.ShapeDtypeStruct((db, H_C, DV_C), F32),
                   jax.ShapeDtypeStruct((db, H_C, DK_C, DV_C), F32)],
        compiler_params=_cparams(("arbitrary",)),
        name="gla_sample",
    )(q_s, k_s, v_s, r_s, gt_s, state, wg2p, b_gate.reshape(1, -1), b_r.reshape(1, -1), gn.reshape(1, -1))


def _swa_head_perm():
    perm = np.zeros((H_B * HD_B,), np.int32)
    for p in range(G_B):
        for kv in range(KV_B):
            for d in range(HD_B):
                perm[p * LANE + kv * HD_B + d] = (kv * G_B + p) * HD_B + d
    return perm


def _sample_rows(hm, np_rows, db):
    return jnp.transpose(hm[:, np_rows:np_rows + db, :], (1, 0, 2)).astype(F32)


def kernel(x_prompt, x_sample, cache_k_a, cache_v_a, state_swa_k, state_swa_v, state_gla, page_table,
           w_qkv_a, lam_a, subln_a, w_o_a, w_qkv_b, b_qkv_b, sinks_b, w_o_b, b_o_b,
           w_in_c, w_gate2_c, b_gate_c, b_r_c, gn_c, w_o_c, ln1_g, ln1_b, ln2_g, ln2_b,
           w_gu_d, w_down_d, w_router, w_gu_e, w_down_e):
    batch, t_len, _ = x_prompt.shape
    db = x_sample.shape[0]
    assert x_sample.shape[1] == 1 and db <= SAMPLE_PAD
    n_p = batch * t_len
    nt = n_p + SAMPLE_PAD
    n_real = n_p + db
    assert nt % TOK_TILE == 0

    y = jnp.concatenate([x_prompt.reshape(n_p, D_MODEL), x_sample.reshape(db, D_MODEL),
                         jnp.zeros((SAMPLE_PAD - db, D_MODEL), F32)], axis=0)
    yb = y.astype(BF16)
    cache_k = cache_k_a.reshape(cache_k_a.shape[:2] + (PAGE_SIZE * KV_A, LANE))
    cache_v = cache_v_a.reshape(cache_v_a.shape[:2] + (PAGE_SIZE * KV_A, LANE))
    w_gu_e2 = w_gu_e.reshape((-1,) + w_gu_e.shape[2:])
    w_down_e2 = w_down_e.reshape((-1,) + w_down_e.shape[2:])
    zero_bias = jnp.zeros((D_MODEL,), F32)
    perm = _swa_head_perm()
    ffn_tile = FFN_TILE if nt % FFN_TILE == 0 else TOK_TILE
    n_ffn_tiles = nt // ffn_tile

    ka_p, va_p, ka_s, va_s = [], [], [], []
    kb_p, vb_p, kb_s, vb_s = [], [], [], []
    gc_p, gc_s = [], []
    for i in range(DEPTH):
        j = i // 3
        if i % 3 == 0:
            nq = H_A * LANE
            (q_hm,) = _proj(yb, w_qkv_a, jnp.zeros((nq,), F32), layer=j, col0=0, ncols=nq, cw=512,
                            hm_dtype=BF16, emit_tok=None)
            kv_hm, k_tok, v_tok = _proj(yb, w_qkv_a, jnp.zeros((nq,), F32), layer=j, col0=nq, ncols=nq, cw=nq,
                                        hm_dtype=BF16, emit_tok="3d")
            ka_p.append(k_tok[:n_p].reshape(batch, t_len, KV_A, LANE))
            va_p.append(v_tok[:n_p].reshape(batch, t_len, KV_A, LANE))
            k_new = k_tok[n_p:n_real]
            v_new = v_tok[n_p:n_real]
            ka_s.append(k_new.reshape(db, 1, KV_A, LANE))
            va_s.append(v_new.reshape(db, 1, KV_A, LANE))
            lam_full, lam_init = _diff_lambda(lam_a[j], i)
            attn_p = _diffattn_prompt(q_hm, kv_hm, lam_full, lam_init, subln_a[j], batch, t_len)
            attn_s = _diffattn_sample(_sample_rows(q_hm, n_p, db), k_new, v_new, cache_k, cache_v, j,
                                      page_table, lam_full, lam_init, subln_a[j]).reshape(db, H_A * LANE)
            w_o, lay_o, b_o = w_o_a, j, zero_bias
        elif i % 3 == 1:
            nq = H_B * HD_B
            w_q = w_qkv_b[j][:, :nq][:, perm][None]
            b_q = b_qkv_b[j][:nq][perm]
            w_kv = w_qkv_b[j][:, nq:][None]
            b_kv = b_qkv_b[j][nq:]
            (q_hm,) = _proj(yb, w_q, b_q, layer=0, col0=0, ncols=nq, cw=512, hm_dtype=BF16, emit_tok=None)
            kv_hm, kv_tok = _proj(yb, w_kv, b_kv, layer=0, col0=0, ncols=2 * LANE, cw=2 * LANE,
                                  hm_dtype=BF16, emit_tok="2d")
            k_tok = kv_tok[:, :LANE]
            v_tok = kv_tok[:, LANE:]
            last_w = lambda a: a[:n_p].reshape(batch, t_len, LANE)[:, t_len - WINDOW:].reshape(
                batch, WINDOW, KV_B, HD_B)
            kb_p.append(last_w(k_tok))
            vb_p.append(last_w(v_tok))
            k_new = k_tok[n_p:n_real].reshape(db, 1, LANE)
            v_new = v_tok[n_p:n_real].reshape(db, 1, LANE)
            k_buf = state_swa_k[j].reshape(db, WINDOW, LANE)
            v_buf = state_swa_v[j].reshape(db, WINDOW, LANE)
            kb_s.append(jnp.concatenate([k_buf[:, 1:], k_new], axis=1).reshape(db, WINDOW, KV_B, HD_B))
            vb_s.append(jnp.concatenate([v_buf[:, 1:], v_new], axis=1).reshape(db, WINDOW, KV_B, HD_B))
            attn_p = _swa_prompt(q_hm, kv_hm, sinks_b[j], batch, t_len)
            attn_s = _swa_sample(_sample_rows(q_hm, n_p, db), k_buf, v_buf, k_new, v_new,
                                 sinks_b[j]).reshape(db, G_B * LANE)
            w_o, lay_o, b_o = w_o_b[j][perm][None], 0, b_o_b[j]
        else:
            n_in = w_in_c.shape[2]
            n_main = 2 * H_C * DK_C + 2 * H_C * DV_C
            w_in = jnp.zeros((1, D_MODEL, n_main + LANE), F32).at[0, :, :n_in].set(w_in_c[j])
            (hm,) = _proj(yb, w_in, jnp.zeros((n_main + LANE,), F32), layer=0, col0=0,
                          ncols=n_main + LANE, cw=5 * LANE, hm_dtype=F32, emit_tok=None)
            wg2p = jnp.zeros((LANE, H_C * DK_C), F32).at[:GATE_RANK].set(w_gate2_c[j])
            attn_p, st_p = _gla_prompt(hm, wg2p, b_gate_c[j], b_r_c[j], gn_c[j], batch, t_len)
            gc_p.append(st_p)
            hs = _sample_rows(hm, n_p, db)
            q_s = hs[:, 0:H_C]
            k_s = hs[:, H_C:2 * H_C]
            v_s = hs[:, 2 * H_C:4 * H_C].reshape(db, H_C, DV_C)
            r_s = hs[:, 4 * H_C:6 * H_C].reshape(db, H_C, DV_C)
            gt_s = hs[:, 6 * H_C:6 * H_C + 1]
            attn_s, st_s = _gla_sample(q_s, k_s, v_s, r_s, gt_s, state_gla[j].astype(F32), wg2p,
                                       b_gate_c[j], b_r_c[j], gn_c[j])
            gc_s.append(st_s)
            attn_s = attn_s.reshape(db, H_C * DV_C)
            w_o, lay_o, b_o = w_o_c, j, zero_bias

        f = i // 2
        if i % 2 == 0:
            y1, y1b = _outproj_ln(attn_p, attn_s, w_o, lay_o, b_o, y, ln1_g[i], ln1_b[i])
            y, yb = _ffn(y1b, w_gu_d, w_down_d, jnp.full((n_ffn_tiles,), f, jnp.int32),
                         jnp.ones((n_ffn_tiles,), jnp.int32), tm=ffn_tile,
                         dense_args=(y1, ln2_g[i], ln2_b[i]))
        else:
            y1, _, route = _outproj_ln(attn_p, attn_s, w_o, lay_o, b_o, y, ln1_g[i], ln1_b[i], w_router[f])
            y, yb = _moe(y1, route, n_real, w_gu_e2, w_down_e2, f * N_EXPERTS, ln2_g[i], ln2_b[i])

    yp = y[:n_p].reshape(batch, t_len, D_MODEL)
    ys = y[n_p:n_real].reshape(db, 1, D_MODEL)
    return (yp, ys, jnp.stack(ka_p), jnp.stack(va_p), jnp.stack(ka_s), jnp.stack(va_s),
            jnp.stack(kb_p), jnp.stack(vb_p), jnp.stack(kb_s), jnp.stack(vb_s),
            jnp.stack(gc_p), jnp.stack(gc_s))
```

```python
import functools
import math

import jax
import jax.numpy as jnp
import numpy as np
from jax import lax
from jax.experimental import pallas as pl
from jax.experimental.pallas import tpu as pltpu

F32 = jnp.float32
BF16 = jnp.bfloat16

D_MODEL = 1024
DEPTH = 4
PAGE_SIZE = 128
H_A, KV_A, G_A, HD_A = 8, 4, 2, 64
H_B, KV_B, G_B, HD_B = 16, 2, 8, 64
WINDOW = 128
H_C, DK_C, DV_C = 4, 128, 256
GATE_RANK = 16
GATE_TAU = 16.0
D_FF = 2816
N_EXPERTS = 8
LN_EPS = 1e-5
RMS_EPS = 1e-6
ALPHA = (2 * DEPTH) ** 0.25

LANE = 128
VMEM_LIMIT = 56 * 1024 * 1024
NEG_INF = float("-inf")

SAMPLE_PAD = 512
TOK_TILE = 768
FF_TILE = 256
MOE_TILE = 1024
FFN_TILE = 1056
Q_TILE = 512
K_TILE = 512
GLA_SUB = 64
GLA_TILE = 256
PAGES_PER_STEP = 16


def _cparams(sem):
    return pltpu.CompilerParams(dimension_semantics=sem, vmem_limit_bytes=VMEM_LIMIT)


def _layer_norm_rows(x, g, b):
    mu = jnp.mean(x, axis=-1, keepdims=True)
    xc = x - mu
    var = jnp.mean(xc * xc, axis=-1, keepdims=True)
    return xc * lax.rsqrt(var + LN_EPS) * g + b


def _proj_kernel(x_ref, w_ref, b_ref, *rest, emit_tok):
    hm_ref, wb_ref = rest[0], rest[-1]
    tok_refs = rest[1:-1]

    @pl.when(pl.program_id(1) == 0)
    def _():
        wb_ref[...] = w_ref[...].astype(BF16)

    y = jnp.dot(x_ref[...], wb_ref[...], preferred_element_type=F32) + b_ref[...]
    for c in range(hm_ref.shape[0]):
        hm_ref[c] = y[:, c * LANE:(c + 1) * LANE].astype(hm_ref.dtype)
        if emit_tok == "3d":
            per = tok_refs[0].shape[1]
            tok_refs[c // per][:, c % per, :] = y[:, c * LANE:(c + 1) * LANE]
    if emit_tok == "2d":
        tok_refs[0][...] = y


def _proj(xb, w, b, *, layer, col0, ncols, cw, hm_dtype, emit_tok, tm=TOK_TILE):
    nt = xb.shape[0]
    assert ncols % cw == 0 and col0 % cw == 0 and cw % LANE == 0 and nt % tm == 0
    nb = cw // LANE
    j0 = col0 // cw
    out_shape = [jax.ShapeDtypeStruct((ncols // LANE, nt, LANE), hm_dtype)]
    out_specs = [pl.BlockSpec((nb, tm, LANE), lambda j, i: (j, i, 0))]
    if emit_tok == "3d":
        assert cw == ncols and nb % KV_A == 0
        for _ in range(nb // KV_A):
            out_shape.append(jax.ShapeDtypeStruct((nt, KV_A, LANE), F32))
            out_specs.append(pl.BlockSpec((tm, KV_A, LANE), lambda j, i: (i, 0, 0)))
    elif emit_tok == "2d":
        out_shape.append(jax.ShapeDtypeStruct((nt, ncols), F32))
        out_specs.append(pl.BlockSpec((tm, cw), lambda j, i: (i, j)))
    return pl.pallas_call(
        functools.partial(_proj_kernel, emit_tok=emit_tok),
        grid=(ncols // cw, nt // tm),
        in_specs=[pl.BlockSpec((tm, D_MODEL), lambda j, i: (i, 0)),
                  pl.BlockSpec((None, D_MODEL, cw), lambda j, i: (layer, 0, j0 + j)),
                  pl.BlockSpec((1, cw), lambda j, i: (0, j))],
        out_specs=out_specs,
        out_shape=out_shape,
        scratch_shapes=[pltpu.VMEM((D_MODEL, cw), BF16)],
        compiler_params=_cparams(("arbitrary", "arbitrary")),
        name="proj",
    )(xb, w, b.reshape(1, ncols))


def _top2_route(logits):
    lane = lax.broadcasted_iota(jnp.int32, logits.shape, 1)
    l1 = jnp.where(lane < N_EXPERTS, logits, NEG_INF)
    m1 = jnp.max(l1, axis=-1, keepdims=True)
    i1 = jnp.min(jnp.where(l1 == m1, lane, LANE), axis=-1, keepdims=True)
    l2 = jnp.where(lane == i1, NEG_INF, l1)
    m2 = jnp.max(l2, axis=-1, keepdims=True)
    i2 = jnp.min(jnp.where(l2 == m2, lane, LANE), axis=-1, keepdims=True)
    e = jnp.exp(m2 - m1)
    g1 = 1.0 / (1.0 + e)
    g2 = e / (1.0 + e)
    return jnp.where(lane == 0, g1,
                     jnp.where(lane == 1, g2,
                               jnp.where(lane == 2, i1.astype(F32),
                                         jnp.where(lane == 3, i2.astype(F32), 0.0))))


def _outproj_kernel(ap_ref, as_ref, w_ref, b_ref, res_ref, g_ref, be_ref, *rest, route):
    if route:
        wr_ref, y_ref, yb_ref, rt_ref, wb_ref = rest
    else:
        y_ref, yb_ref, wb_ref = rest

    @pl.when(pl.program_id(0) == 0)
    def _():
        wb_ref[...] = w_ref[...].astype(BF16)

    a = jnp.where(pl.program_id(0) == pl.num_programs(0) - 1, as_ref[...], ap_ref[...])
    m = jnp.dot(a, wb_ref[...], preferred_element_type=F32) + b_ref[...]
    y = _layer_norm_rows(ALPHA * res_ref[...] + m, g_ref[...], be_ref[...])
    y_ref[...] = y
    yb_ref[...] = y.astype(BF16)
    if route:
        logits = jnp.dot(y, wr_ref[...], preferred_element_type=F32, precision=lax.Precision.HIGHEST)
        rt_ref[...] = _top2_route(logits)


def _outproj_ln(a_p, a_s, w, layer, b, res, g, be, w_router=None):
    tm = SAMPLE_PAD
    n_p, k = a_p.shape
    nt = res.shape[0]
    assert n_p % tm == 0 and nt == n_p + tm
    a_s = jnp.concatenate([a_s.astype(BF16), jnp.zeros((tm - a_s.shape[0], k), BF16)], axis=0)
    n_ptiles = n_p // tm
    route = w_router is not None
    row = lambda i: (i, 0)
    const = lambda i: (0, 0)
    in_specs = [pl.BlockSpec((tm, k), lambda i: (jnp.minimum(i, n_ptiles - 1), 0)),
                pl.BlockSpec((tm, k), const), pl.BlockSpec((None, k, D_MODEL), lambda i: (layer, 0, 0)),
                pl.BlockSpec((1, D_MODEL), const), pl.BlockSpec((tm, D_MODEL), row),
                pl.BlockSpec((1, D_MODEL), const), pl.BlockSpec((1, D_MODEL), const)]
    args = [a_p, a_s, w, b.reshape(1, D_MODEL), res, g.reshape(1, D_MODEL), be.reshape(1, D_MODEL)]
    out_shape = [jax.ShapeDtypeStruct((nt, D_MODEL), F32), jax.ShapeDtypeStruct((nt, D_MODEL), BF16)]
    out_specs = [pl.BlockSpec((tm, D_MODEL), row), pl.BlockSpec((tm, D_MODEL), row)]
    if route:
        wr = jnp.zeros((D_MODEL, LANE), F32).at[:, :N_EXPERTS].set(w_router)
        in_specs.append(pl.BlockSpec((D_MODEL, LANE), const))
        args.append(wr)
        out_shape.append(jax.ShapeDtypeStruct((nt, LANE), F32))
        out_specs.append(pl.BlockSpec((tm, LANE), row))
    return pl.pallas_call(
        functools.partial(_outproj_kernel, route=route),
        grid=(nt // tm,),
        in_specs=in_specs, out_specs=out_specs, out_shape=out_shape,
        scratch_shapes=[pltpu.VMEM((k, D_MODEL), BF16)],
        compiler_params=_cparams(("arbitrary",)),
        name="outproj_ln",
    )(*args)


def _ffn_kernel(te_ref, tv_ref, x_ref, wg_ref, wu_ref, wd_ref, *rest, dense):
    if dense:
        res_ref, g_ref, be_ref, y_ref, yb_ref, acc_ref = rest
    else:
        y_ref, xb_ref = rest
        acc_ref = y_ref
    i = pl.program_id(0)
    j = pl.program_id(1)
    nj = pl.num_programs(1)

    @pl.when(j == 0)
    def _():
        acc_ref[...] = jnp.zeros(acc_ref.shape, F32)

    @pl.when(tv_ref[i] > 0)
    def _():
        if dense:
            x = x_ref[...]
        else:
            @pl.when(j == 0)
            def _():
                xb_ref[...] = x_ref[...].astype(BF16)
            x = xb_ref[...]
        gate = jnp.dot(x, wg_ref[...].astype(BF16), preferred_element_type=F32)
        up = jnp.dot(x, wu_ref[...].astype(BF16), preferred_element_type=F32)
        h = (gate * (1.0 / (1.0 + jnp.exp(-gate))) * up).astype(BF16)
        acc_ref[...] += jnp.dot(h, wd_ref[...].astype(BF16), preferred_element_type=F32)

        if dense:
            @pl.when(j == nj - 1)
            def _():
                y = _layer_norm_rows(ALPHA * res_ref[...] + acc_ref[...], g_ref[...], be_ref[...])
                y_ref[...] = y
                yb_ref[...] = y.astype(BF16)


def _ffn(xb, w_gu, w_down, tile_expert, tile_valid, *, tm, dense_args=None, tf=FF_TILE):
    m = xb.shape[0]
    assert m % tm == 0 and D_FF % tf == 0
    nf = D_FF // tf
    dense = dense_args is not None

    def jeff(i, j, tv):
        return jnp.where(tv[i] > 0, j, nf - 1)

    in_specs = [
        pl.BlockSpec((tm, xb.shape[1]), lambda i, j, te, tv: (i, 0)),
        pl.BlockSpec((None, D_MODEL, tf), lambda i, j, te, tv: (te[i], 0, jeff(i, j, tv))),
        pl.BlockSpec((None, D_MODEL, tf), lambda i, j, te, tv: (te[i], 0, nf + jeff(i, j, tv))),
        pl.BlockSpec((None, tf, D_MODEL), lambda i, j, te, tv: (te[i], jeff(i, j, tv), 0)),
    ]
    args = [xb, w_gu, w_gu, w_down]
    row = lambda i, j, te, tv: (i, 0)
    const = lambda i, j, te, tv: (0, 0)
    if dense:
        res, g, be = dense_args
        in_specs += [pl.BlockSpec((tm, D_MODEL), row), pl.BlockSpec((1, D_MODEL), const),
                     pl.BlockSpec((1, D_MODEL), const)]
        args += [res, g.reshape(1, D_MODEL), be.reshape(1, D_MODEL)]
        out_shape = [jax.ShapeDtypeStruct((m, D_MODEL), F32), jax.ShapeDtypeStruct((m, D_MODEL), BF16)]
        out_specs = [pl.BlockSpec((tm, D_MODEL), row), pl.BlockSpec((tm, D_MODEL), row)]
    else:
        out_shape = jax.ShapeDtypeStruct((m, D_MODEL), F32)
        out_specs = pl.BlockSpec((tm, D_MODEL), row)
    scratch = [pltpu.VMEM((tm, D_MODEL), F32) if dense else pltpu.VMEM((tm, D_MODEL), BF16)]
    return pl.pallas_call(
        functools.partial(_ffn_kernel, dense=dense),
        grid_spec=pltpu.PrefetchScalarGridSpec(
            num_scalar_prefetch=2, grid=(m // tm, nf),
            in_specs=in_specs, out_specs=out_specs, scratch_shapes=scratch),
        out_shape=out_shape,
        compiler_params=_cparams(("arbitrary", "arbitrary")),
        name="ffn_dense" if dense else "ffn_moe",
    )(tile_expert, tile_valid, *args)


def _combine_kernel(ya_ref, yb_ref, rt_ref, res_ref, g_ref, be_ref, y_ref, ybf_ref):
    rt = rt_ref[...]
    f = rt[:, 0:1] * ya_ref[...] + rt[:, 1:2] * yb_ref[...]
    y = _layer_norm_rows(ALPHA * res_ref[...] + f, g_ref[...], be_ref[...])
    y_ref[...] = y
    ybf_ref[...] = y.astype(BF16)


def _combine_ln(ya, yb, route, res, g, be, *, tm=TOK_TILE):
    nt = res.shape[0]
    row = lambda i: (i, 0)
    const = lambda i: (0, 0)
    return pl.pallas_call(
        _combine_kernel,
        grid=(nt // tm,),
        in_specs=[pl.BlockSpec((tm, D_MODEL), row), pl.BlockSpec((tm, D_MODEL), row),
                  pl.BlockSpec((tm, LANE), row), pl.BlockSpec((tm, D_MODEL), row),
                  pl.BlockSpec((1, D_MODEL), const), pl.BlockSpec((1, D_MODEL), const)],
        out_specs=[pl.BlockSpec((tm, D_MODEL), row), pl.BlockSpec((tm, D_MODEL), row)],
        out_shape=[jax.ShapeDtypeStruct((nt, D_MODEL), F32), jax.ShapeDtypeStruct((nt, D_MODEL), BF16)],
        compiler_params=_cparams(("arbitrary",)),
        name="moe_combine_ln",
    )(ya, yb, route, res, g.reshape(1, D_MODEL), be.reshape(1, D_MODEL))


def _moe(res, route, n_real, w_gu, w_down, e_off, g, be, *, tm=MOE_TILE):
    nt = res.shape[0]
    gates_idx = route[:n_real, 2:4].astype(jnp.int32)
    flat_e = gates_idx.reshape(-1)
    npair = flat_e.shape[0]
    onehot = (flat_e[:, None] == jnp.arange(N_EXPERTS)[None, :]).astype(jnp.int32)
    csum = jnp.cumsum(onehot, axis=0)
    counts = csum[-1]
    rank = jnp.take_along_axis(csum, flat_e[:, None], axis=1)[:, 0] - 1
    tiles_per = (counts + tm - 1) // tm
    tile_end = jnp.cumsum(tiles_per)
    tile_start = tile_end - tiles_per
    n_tiles = npair // tm + N_EXPERTS
    m_pad = n_tiles * tm
    dest = tile_start[flat_e] * tm + rank
    row_src = jnp.zeros((m_pad,), jnp.int32).at[dest].set(
        jnp.arange(npair, dtype=jnp.int32) // 2, unique_indices=True, mode="promise_in_bounds")
    t_ids = jnp.arange(n_tiles, dtype=jnp.int32)
    tile_valid = (t_ids < tile_end[-1]).astype(jnp.int32)
    tile_expert = jnp.sum((t_ids[:, None] >= tile_end[None, :]).astype(jnp.int32), axis=1)
    tile_expert = jnp.minimum(tile_expert, N_EXPERTS - 1)
    last_e = tile_expert[jnp.maximum(tile_end[-1] - 1, 0)]
    tile_expert = jnp.where(tile_valid > 0, tile_expert, last_e) + e_off
    x_sorted = res.at[row_src].get(mode="promise_in_bounds")
    y_sorted = _ffn(x_sorted, w_gu, w_down, tile_expert, tile_valid, tm=tm)
    dest2 = jnp.zeros((nt, 2), jnp.int32).at[:n_real].set(dest.reshape(n_real, 2))
    ya = y_sorted.at[dest2[:, 0]].get(mode="promise_in_bounds")
    ybb = y_sorted.at[dest2[:, 1]].get(mode="promise_in_bounds")
    return _combine_ln(ya, ybb, route, res, g, be)


def _diff_lambda(lam, layer_idx):
    lam_init = 0.8 - 0.6 * math.exp(-0.3 * layer_idx)
    lf = lam.astype(F32)
    lam_full = jnp.exp(jnp.sum(lf[0] * lf[1])) - jnp.exp(jnp.sum(lf[2] * lf[3])) + lam_init
    return lam_full, lam_init


def _alibi_slopes(n):
    return [2.0 ** (-8.0 * (i + 1) / n) for i in range(n)]


def _sub_rms(o, subln, out_scale):
    return o * lax.rsqrt(jnp.mean(o * o, axis=-1, keepdims=True) + RMS_EPS) * subln * out_scale


def _diffattn_prompt_kernel(q_ref, k_ref, v_ref, pos_ref, slope_ref, lam_ref, subln_ref, o_ref,
                            kaug_ref, vt_ref, qaug_ref, m_ref, l_ref, acc_ref, *, out_scale):
    qi = pl.program_id(2)
    tq = q_ref.shape[1]
    t_len = k_ref.shape[0]
    nrow = 2 * G_A * tq

    @pl.when(qi == 0)
    def _():
        kaug_ref[:, :LANE] = k_ref[...]
        kaug_ref[:, LANE:] = pos_ref[...]
        for jb in range(t_len // K_TILE):
            vt_ref[jb] = v_ref[jb * K_TILE:(jb + 1) * K_TILE, :].astype(F32).T.astype(BF16)

    lane = lax.broadcasted_iota(jnp.int32, (tq, LANE), 1)
    scale = HD_A ** -0.5
    for g in range(G_A):
        qg = q_ref[g] * scale
        qaug_ref[(2 * g) * tq:(2 * g + 1) * tq, :LANE] = jnp.where(lane < HD_A, qg, 0).astype(BF16)
        qaug_ref[(2 * g + 1) * tq:(2 * g + 2) * tq, :LANE] = jnp.where(lane >= HD_A, qg, 0).astype(BF16)
    qaug_ref[:, LANE:] = slope_ref[...]

    m_ref[...] = jnp.full(m_ref.shape, NEG_INF, F32)
    l_ref[...] = jnp.zeros(l_ref.shape, F32)
    acc_ref[...] = jnp.zeros(acc_ref.shape, F32)

    def block(j, masked):
        start = pl.multiple_of(j * K_TILE, K_TILE)
        kb = kaug_ref[pl.ds(start, K_TILE), :]
        s = lax.dot_general(kb, qaug_ref[...], (((1,), (1,)), ((), ())), preferred_element_type=F32)
        if masked:
            key = lax.broadcasted_iota(jnp.int32, (K_TILE, nrow), 0)
            col = lax.broadcasted_iota(jnp.int32, (K_TILE, nrow), 1)
            s = jnp.where(start + key <= qi * tq + (col & (tq - 1)), s, NEG_INF)
        m_old = m_ref[...]
        m_new = jnp.maximum(m_old, jnp.max(s, axis=0, keepdims=True))
        alpha = jnp.exp(m_old - m_new)
        p = jnp.exp(s - m_new)
        l_ref[...] = alpha * l_ref[...] + jnp.sum(p, axis=0, keepdims=True)
        acc_ref[...] = alpha * acc_ref[...] + jnp.dot(vt_ref[j], p.astype(BF16), preferred_element_type=F32)
        m_ref[...] = m_new

    n_full = (qi * tq) // K_TILE

    def body(j, carry):
        block(j, False)
        return carry

    lax.fori_loop(0, n_full, body, 0)
    block(n_full, True)

    lam = lam_ref[...]
    norm = acc_ref[...] / l_ref[...]
    for g in range(G_A):
        o = norm[:, (2 * g) * tq:(2 * g + 1) * tq].T - lam * norm[:, (2 * g + 1) * tq:(2 * g + 2) * tq].T
        o_ref[:, g * LANE:(g + 1) * LANE] = _sub_rms(o, subln_ref[...], out_scale).astype(o_ref.dtype)


def _alibi_aug_consts(t_len, tq):
    pos = np.arange(t_len)
    kcols = np.zeros((t_len, LANE), np.float32)
    kcols[:, 0] = (pos // 64) * 64
    kcols[:, 1] = pos % 64
    slopes = _alibi_slopes(H_A)
    qcols = np.zeros((KV_A, 2 * G_A * tq, LANE), np.float32)
    for kv in range(KV_A):
        for g in range(G_A):
            qcols[kv, 2 * g * tq:(2 * g + 2) * tq, 0:2] = slopes[kv * G_A + g]
    return jnp.asarray(kcols, BF16), jnp.asarray(qcols, BF16)


def _diffattn_prompt(q_hm, kv_hm, lam_full, lam_init, subln, batch, t_len):
    tq = Q_TILE
    nq = t_len // tq
    assert t_len % K_TILE == 0
    kcols, qcols = _alibi_aug_consts(t_len, tq)
    nrow = 2 * G_A * tq
    lam_row = jnp.full((1, LANE), lam_full, F32)
    return pl.pallas_call(
        functools.partial(_diffattn_prompt_kernel, out_scale=1.0 - lam_init),
        grid=(batch, KV_A, nq),
        in_specs=[pl.BlockSpec((G_A, tq, LANE), lambda b, kv, qi: (kv, b * nq + qi, 0)),
                  pl.BlockSpec((None, t_len, LANE), lambda b, kv, qi: (kv, b, 0)),
                  pl.BlockSpec((None, t_len, LANE), lambda b, kv, qi: (KV_A + kv, b, 0)),
                  pl.BlockSpec((t_len, LANE), lambda b, kv, qi: (0, 0)),
                  pl.BlockSpec((None, nrow, LANE), lambda b, kv, qi: (kv, 0, 0)),
                  pl.BlockSpec((1, LANE), lambda b, kv, qi: (0, 0)),
                  pl.BlockSpec((1, LANE), lambda b, kv, qi: (0, 0))],
        out_specs=pl.BlockSpec((tq, G_A * LANE), lambda b, kv, qi: (b * nq + qi, kv)),
        out_shape=jax.ShapeDtypeStruct((batch * t_len, H_A * LANE), BF16),
        scratch_shapes=[pltpu.VMEM((t_len, 2 * LANE), BF16), pltpu.VMEM((t_len // K_TILE, LANE, K_TILE), BF16),
                        pltpu.VMEM((nrow, 2 * LANE), BF16),
                        pltpu.VMEM((1, nrow), F32), pltpu.VMEM((1, nrow), F32), pltpu.VMEM((LANE, nrow), F32)],
        compiler_params=_cparams(("arbitrary", "arbitrary", "arbitrary")),
        name="diffattn_prompt",
    )(q_hm, kv_hm, kv_hm, kcols, qcols, lam_row, subln.reshape(1, LANE))


def _diffattn_sample_kernel(pt_ref, q_ref, kn_ref, vn_ref, lam_ref, subln_ref, slope_ref, *rest,
                            out_scale, past_len):
    pp = PAGES_PER_STEP
    k_refs = rest[:pp]
    v_refs = rest[pp:2 * pp]
    o_ref, m_ref, l_ref, acc_ref = rest[2 * pp:]
    c = pl.program_id(1)
    nc = pl.num_programs(1)
    nrow = KV_A * 8
    ncol = pp * PAGE_SIZE * KV_A

    @pl.when(c == 0)
    def _():
        m_ref[...] = jnp.full(m_ref.shape, NEG_INF, F32)
        l_ref[...] = jnp.zeros(l_ref.shape, F32)
        acc_ref[...] = jnp.zeros(acc_ref.shape, F32)

    kc = jnp.concatenate([r[...] for r in k_refs], axis=0).astype(BF16)
    vc = jnp.concatenate([r[...] for r in v_refs], axis=0).astype(BF16)
    s = lax.dot_general(q_ref[...].astype(BF16), kc, (((1,), (1,)), ((), ())), preferred_element_type=F32)
    row = lax.broadcasted_iota(jnp.int32, (nrow, ncol), 0)
    col = lax.broadcasted_iota(jnp.int32, (nrow, ncol), 1)
    pos = c * (pp * PAGE_SIZE) + (col >> 2)
    dist = past_len - pos.astype(F32)
    s = jnp.where((col & (KV_A - 1)) == (row >> 3), s - slope_ref[...] * dist, NEG_INF)
    m_old = m_ref[...]
    m_new = jnp.maximum(m_old, jnp.max(s, axis=-1, keepdims=True))
    alpha = jnp.exp(m_old - m_new)
    p = jnp.exp(s - m_new)
    l_ref[...] = alpha * l_ref[...] + jnp.sum(p, axis=-1, keepdims=True)
    acc_ref[...] = alpha * acc_ref[...] + jnp.dot(p.astype(BF16), vc, preferred_element_type=F32)
    m_ref[...] = m_new

    @pl.when(c == nc - 1)
    def _():
        lam = lam_ref[...]
        s_self = jnp.sum(q_ref[...] * kn_ref[...], axis=-1, keepdims=True)
        m_old = m_ref[...]
        m_fin = jnp.maximum(m_old, s_self)
        alpha = jnp.exp(m_old - m_fin)
        p_self = jnp.exp(s_self - m_fin)
        norm = (alpha * acc_ref[...] + p_self * vn_ref[...]) / (alpha * l_ref[...] + p_self)
        for kv in range(KV_A):
            for g in range(G_A):
                o = norm[kv * 8 + g:kv * 8 + g + 1] - lam * norm[kv * 8 + 4 + g:kv * 8 + 5 + g]
                o_ref[kv * G_A + g:kv * G_A + g + 1, :] = _sub_rms(o, subln_ref[...], out_scale)


def _diffattn_sample(q_s, k_new, v_new, cache_k, cache_v, layer_j, page_table, lam_full, lam_init, subln):
    db, n_pages = page_table.shape
    nrow = KV_A * 8
    qg = q_s.reshape(db, KV_A, G_A, LANE) * (HD_A ** -0.5)
    lo = jnp.arange(LANE) < HD_A
    zero = jnp.zeros((db, KV_A, 2, LANE), F32)
    q32 = jnp.concatenate([jnp.where(lo, qg, 0.0), zero, jnp.where(lo, 0.0, qg), zero], axis=2)
    q32 = q32.reshape(db, nrow, LANE)
    kn32 = jnp.repeat(k_new, 8, axis=1)
    vn32 = jnp.repeat(v_new, 8, axis=1)
    pp = PAGES_PER_STEP
    assert n_pages % pp == 0 and KV_A == 4
    past_len = n_pages * PAGE_SIZE
    slopes = np.zeros((nrow, 1), np.float32)
    sl = _alibi_slopes(H_A)
    for kv in range(KV_A):
        for g in range(G_A):
            slopes[kv * 8 + g, 0] = sl[kv * G_A + g]
            slopes[kv * 8 + 4 + g, 0] = sl[kv * G_A + g]
    lam_row = jnp.full((1, LANE), lam_full, F32)

    def page_spec(i):
        return pl.BlockSpec((None, None, PAGE_SIZE * KV_A, LANE),
                            lambda b, c, pt: (layer_j, pt[b, c * pp + i], 0, 0))

    vec = pl.BlockSpec((None, nrow, LANE), lambda b, c, pt: (b, 0, 0))
    const2 = lambda shape: pl.BlockSpec(shape, lambda b, c, pt: (0, 0))
    in_specs = ([vec, vec, vec, const2((1, LANE)), const2((1, LANE)), const2((nrow, 1))]
                + [page_spec(i) for i in range(pp)] * 2)
    return pl.pallas_call(
        functools.partial(_diffattn_sample_kernel, out_scale=1.0 - lam_init, past_len=float(past_len)),
        grid_spec=pltpu.PrefetchScalarGridSpec(
            num_scalar_prefetch=1, grid=(db, n_pages // pp),
            in_specs=in_specs,
            out_specs=pl.BlockSpec((None, H_A, LANE), lambda b, c, pt: (b, 0, 0)),
            scratch_shapes=[pltpu.VMEM((nrow, 1), F32), pltpu.VMEM((nrow, 1), F32),
                            pltpu.VMEM((nrow, LANE), F32)]),
        out_shape=jax.ShapeDtypeStruct((db, H_A, LANE), F32),
        compiler_params=_cparams(("arbitrary", "arbitrary")),
        name="diffattn_sample",
    )(page_table, q32, kn32, vn32, lam_row, subln.reshape(1, LANE), jnp.asarray(slopes),
      *([cache_k] * pp), *([cache_v] * pp))


def _swa_prompt_kernel(sink_ref, q_ref, kp_ref, kc_ref, vp_ref, vc_ref, o_ref):
    n = pl.program_id(1)
    w = WINDOW
    kband = jnp.concatenate([kp_ref[...], kc_ref[...]], axis=0)
    vband = jnp.concatenate([vp_ref[...], vc_ref[...]], axis=0)
    lane = lax.broadcasted_iota(jnp.int32, (w, LANE), 1)
    row = lax.broadcasted_iota(jnp.int32, (w, 2 * w), 0)
    col = lax.broadcasted_iota(jnp.int32, (w, 2 * w), 1)
    rel = (w + row) - col
    valid = jnp.logical_and(jnp.logical_and(rel >= 0, rel <= w), jnp.logical_or(col >= w, n > 0))
    relf = rel.astype(F32)
    slopes = _alibi_slopes(H_B)
    scale = HD_B ** -0.5
    o_kv = []
    for kv in range(KV_B):
        keep = (lane < HD_B) if kv == 0 else (lane >= HD_B)
        qs = jnp.concatenate([jnp.where(keep, q_ref[p] * scale, 0).astype(BF16) for p in range(G_B)], axis=0)
        s_all = lax.dot_general(qs, kband, (((1,), (1,)), ((), ())), preferred_element_type=F32)
        ps = []
        for p in range(G_B):
            h = kv * G_B + p
            s = jnp.where(valid, s_all[p * w:(p + 1) * w] - slopes[h] * relf, NEG_INF)
            sink = sink_ref[h]
            m = jnp.maximum(jnp.max(s, axis=-1, keepdims=True), sink)
            e = jnp.exp(s - m)
            denom = jnp.sum(e, axis=-1, keepdims=True) + jnp.exp(sink - m)
            ps.append((e / denom).astype(BF16))
        o_kv.append(jnp.dot(jnp.concatenate(ps, axis=0), vband, preferred_element_type=F32))
    for p in range(G_B):
        o = jnp.where(lane < HD_B, o_kv[0][p * w:(p + 1) * w], o_kv[1][p * w:(p + 1) * w])
        o_ref[:, p * LANE:(p + 1) * LANE] = o.astype(o_ref.dtype)


def _swa_prompt(q_hm, kv_hm, sinks, batch, t_len):
    w = WINDOW
    nb = t_len // w
    cur = lambda b, n, s: (b * nb + n, 0)
    prev = lambda b, n, s: (b * nb + jnp.maximum(n - 1, 0), 0)
    return pl.pallas_call(
        _swa_prompt_kernel,
        grid_spec=pltpu.PrefetchScalarGridSpec(
            num_scalar_prefetch=1, grid=(batch, nb),
            in_specs=[pl.BlockSpec((G_B, w, LANE), lambda b, n, s: (0,) + cur(b, n, s)),
                      pl.BlockSpec((None, w, LANE), lambda b, n, s: (0,) + prev(b, n, s)),
                      pl.BlockSpec((None, w, LANE), lambda b, n, s: (0,) + cur(b, n, s)),
                      pl.BlockSpec((None, w, LANE), lambda b, n, s: (1,) + prev(b, n, s)),
                      pl.BlockSpec((None, w, LANE), lambda b, n, s: (1,) + cur(b, n, s))],
            out_specs=pl.BlockSpec((w, G_B * LANE), cur)),
        out_shape=jax.ShapeDtypeStruct((batch * t_len, G_B * LANE), BF16),
        compiler_params=_cparams(("arbitrary", "arbitrary")),
        name="swa_prompt",
    )(sinks.astype(F32), q_hm, kv_hm, kv_hm, kv_hm, kv_hm)


def _swa_sample_kernel(q_ref, kb_ref, vb_ref, kn_ref, vn_ref, slope_ref, sink_ref, o_ref):
    w = WINDOW
    q = q_ref[...]
    s = lax.dot_general(q.astype(BF16), kb_ref[...].astype(BF16), (((1,), (1,)), ((), ())),
                        preferred_element_type=F32)
    dist = (w - lax.broadcasted_iota(jnp.int32, (H_B, w), 1)).astype(F32)
    s = s - slope_ref[...] * dist
    s_self = jnp.sum(q * kn_ref[...], axis=-1, keepdims=True)
    sink = sink_ref[...]
    m = jnp.maximum(jnp.maximum(jnp.max(s, axis=-1, keepdims=True), s_self), sink)
    e = jnp.exp(s - m)
    e_self = jnp.exp(s_self - m)
    denom = jnp.sum(e, axis=-1, keepdims=True) + e_self + jnp.exp(sink - m)
    o = (jnp.dot(e.astype(BF16), vb_ref[...].astype(BF16), preferred_element_type=F32)
         + e_self * vn_ref[...]) / denom
    lane = lax.broadcasted_iota(jnp.int32, (G_B, LANE), 1)
    o_ref[...] = jnp.where(lane < HD_B, o[:G_B], o[G_B:])


def _swa_sample(q_s, k_buf, v_buf, k_new, v_new, sinks):
    db = q_s.shape[0]
    lo = jnp.arange(LANE) < HD_B
    qs = q_s * (HD_B ** -0.5)
    q16 = jnp.concatenate([jnp.where(lo, qs, 0.0), jnp.where(lo, 0.0, qs)], axis=1)
    slopes = jnp.asarray(np.asarray(_alibi_slopes(H_B), np.float32).reshape(H_B, 1))
    per_b3 = lambda shape: pl.BlockSpec(shape, lambda b: (b, 0, 0))
    const = pl.BlockSpec((H_B, 1), lambda b: (0, 0))
    return pl.pallas_call(
        _swa_sample_kernel,
        grid=(db,),
        in_specs=[per_b3((None, H_B, LANE)), per_b3((None, WINDOW, LANE)), per_b3((None, WINDOW, LANE)),
                  per_b3((None, 1, LANE)), per_b3((None, 1, LANE)), const, const],
        out_specs=per_b3((None, G_B, LANE)),
        out_shape=jax.ShapeDtypeStruct((db, G_B, LANE), F32),
        compiler_params=_cparams(("arbitrary",)),
        name="swa_sample",
    )(q16, k_buf, v_buf, k_new, v_new, slopes, sinks.astype(F32).reshape(H_B, 1))


def _log_sigmoid(z):
    return jnp.minimum(z, 0.0) - jnp.log(1.0 + jnp.exp(-jnp.abs(z)))


def _silu(x):
    return x * (1.0 / (1.0 + jnp.exp(-x)))


def _gla_prompt_kernel(q_ref, k_ref, v_ref, r_ref, gt_ref, wg2_ref, bg_ref, br_ref, gn_ref,
                       o_ref, st_ref, s_ref):
    ct = pl.program_id(1)
    c = GLA_SUB
    rows_per_step = gt_ref.shape[0]

    @pl.when(ct == 0)
    def _():
        s_ref[...] = jnp.zeros(s_ref.shape, F32)

    z = jnp.dot(gt_ref[...], wg2_ref[...], preferred_element_type=F32,
                precision=lax.Precision.HIGHEST) + bg_ref[...]
    g_all = _log_sigmoid(z) * (1.0 / GATE_TAU)
    ri = lax.broadcasted_iota(jnp.int32, (c, c), 0)
    ci = lax.broadcasted_iota(jnp.int32, (c, c), 1)
    causal = ri >= ci
    tri = causal.astype(F32)
    for sub in range(rows_per_step // c):
        r0 = sub * c
        for h in range(H_C):
            g = g_all[r0:r0 + c, h * DK_C:(h + 1) * DK_C]
            bc = jnp.dot(tri, g, preferred_element_type=F32, precision=lax.Precision.HIGHEST)
            q = q_ref[h, r0:r0 + c, :] * (DK_C ** -0.5)
            k = k_ref[h, r0:r0 + c, :]
            v = jnp.concatenate([v_ref[2 * h, r0:r0 + c, :], v_ref[2 * h + 1, r0:r0 + c, :]], axis=1)
            vb = v.astype(BF16)
            q_dec = (q * jnp.exp(bc)).astype(BF16)
            k_dec = (k * jnp.exp(-bc)).astype(BF16)
            att = lax.dot_general(q_dec, k_dec, (((1,), (1,)), ((), ())), preferred_element_type=F32)
            att = jnp.where(causal, att, 0.0).astype(BF16)
            state = s_ref[h]
            o = (jnp.dot(q_dec, state.astype(BF16), preferred_element_type=F32)
                 + jnp.dot(att, vb, preferred_element_type=F32))
            b_last = bc[c - 1:c, :]
            k_tail = (k * jnp.exp(b_last - bc)).astype(BF16)
            decay_col = jnp.transpose(jnp.broadcast_to(jnp.exp(b_last), (8, DK_C)))[:, 0:1]
            s_ref[h] = decay_col * state + lax.dot_general(
                k_tail, vb, (((0,), (0,)), ((), ())), preferred_element_type=F32)
            r = jnp.concatenate([r_ref[2 * h, r0:r0 + c, :], r_ref[2 * h + 1, r0:r0 + c, :]], axis=1)
            r = _silu(r + br_ref[:, h * DV_C:(h + 1) * DV_C])
            on = o * lax.rsqrt(jnp.mean(o * o, axis=-1, keepdims=True) + RMS_EPS) * gn_ref[...]
            o_ref[r0:r0 + c, h * DV_C:(h + 1) * DV_C] = (on * r).astype(o_ref.dtype)

    @pl.when(ct == pl.num_programs(1) - 1)
    def _():
        st_ref[...] = s_ref[...]


def _gla_prompt(hm, wg2p, b_gate, b_r, gn, batch, t_len):
    rt = GLA_TILE
    nc = t_len // rt
    rowblk = lambda b, ct: b * nc + ct
    const = lambda b, ct: (0, 0)
    return pl.pallas_call(
        _gla_prompt_kernel,
        grid=(batch, nc),
        in_specs=[pl.BlockSpec((H_C, rt, LANE), lambda b, ct: (0, rowblk(b, ct), 0)),
                  pl.BlockSpec((H_C, rt, LANE), lambda b, ct: (1, rowblk(b, ct), 0)),
                  pl.BlockSpec((2 * H_C, rt, LANE), lambda b, ct: (1, rowblk(b, ct), 0)),
                  pl.BlockSpec((2 * H_C, rt, LANE), lambda b, ct: (2, rowblk(b, ct), 0)),
                  pl.BlockSpec((None, rt, LANE), lambda b, ct: (6 * H_C, rowblk(b, ct), 0)),
                  pl.BlockSpec((LANE, H_C * DK_C), const), pl.BlockSpec((1, H_C * DK_C), const),
                  pl.BlockSpec((1, H_C * DV_C), const), pl.BlockSpec((1, DV_C), const)],
        out_specs=[pl.BlockSpec((rt, H_C * DV_C), lambda b, ct: (rowblk(b, ct), 0)),
                   pl.BlockSpec((None, H_C, DK_C, DV_C), lambda b, ct: (b, 0, 0, 0))],
        out_shape=[jax.ShapeDtypeStruct((batch * t_len, H_C * DV_C), BF16),
                   jax.ShapeDtypeStruct((batch, H_C, DK_C, DV_C), F32)],
        scratch_shapes=[pltpu.VMEM((H_C, DK_C, DV_C), F32)],
        compiler_params=_cparams(("arbitrary", "arbitrary")),
        name="gla_prompt",
    )(hm, hm, hm, hm, hm, wg2p, b_gate.reshape(1, -1), b_r.reshape(1, -1), gn.reshape(1, -1))


def _gla_sample_kernel(q_ref, k_ref, v_ref, r_ref, gt_ref, s_ref, wg2_ref, bg_ref, br_ref, gn_ref,
                       o_ref, so_ref):
    hi = lax.Precision.HIGHEST
    z = jnp.dot(jnp.broadcast_to(gt_ref[...], (8, LANE)), wg2_ref[...], preferred_element_type=F32,
                precision=hi)[0:1] + bg_ref[...]
    g_all = _log_sigmoid(z) * (1.0 / GATE_TAU)
    ri = lax.broadcasted_iota(jnp.int32, (DK_C, DK_C), 0)
    ci = lax.broadcasted_iota(jnp.int32, (DK_C, DK_C), 1)
    eye = ri == ci
    for h in range(H_C):
        eg = jnp.exp(g_all[:, h * DK_C:(h + 1) * DK_C])
        q = q_ref[h:h + 1, :] * (DK_C ** -0.5)
        k = k_ref[h:h + 1, :]
        v = v_ref[h:h + 1, :]
        state = s_ref[h]
        lhs = jnp.concatenate([jnp.where(eye, jnp.broadcast_to(eg, (DK_C, DK_C)), 0.0),
                               jnp.where(eye, jnp.broadcast_to(k, (DK_C, DK_C)), 0.0)], axis=1)
        rhs = jnp.concatenate([state, jnp.broadcast_to(v, (DK_C, DV_C))], axis=0)
        so_ref[h] = jnp.dot(lhs, rhs, preferred_element_type=F32, precision=hi)
        qd = q * eg
        o = (jnp.dot(jnp.broadcast_to(qd, (8, DK_C)).astype(BF16), state.astype(BF16),
                     preferred_element_type=F32)[0:1]
             + jnp.sum(q * k, axis=-1, keepdims=True) * v)
        r = _silu(r_ref[h:h + 1, :] + br_ref[:, h * DV_C:(h + 1) * DV_C])
        on = o * lax.rsqrt(jnp.mean(o * o, axis=-1, keepdims=True) + RMS_EPS) * gn_ref[...]
        o_ref[h:h + 1, :] = on * r


def _gla_sample(q_s, k_s, v_s, r_s, gt_s, state, wg2p, b_gate, b_r, gn):
    db = q_s.shape[0]
    b3 = lambda shape: pl.BlockSpec(shape, lambda b: (b, 0, 0))
    b4 = lambda shape: pl.BlockSpec(shape, lambda b: (b, 0, 0, 0))
    const = lambda shape: pl.BlockSpec(shape, lambda b: (0, 0))
    return pl.pallas_call(
        _gla_sample_kernel,
        grid=(db,),
        in_specs=[b3((None, H_C, DK_C)), b3((None, H_C, DK_C)), b3((None, H_C, DV_C)), b3((None, H_C, DV_C)),
                  b3((None, 1, LANE)), b4((None, H_C, DK_C, DV_C)),
                  const((LANE, H_C * DK_C)), const((1, H_C * DK_C)), const((1, H_C * DV_C)), const((1, DV_C))],
        out_specs=[b3((None, H_C, DV_C)), b4((None, H_C, DK_C, DV_C))],
        out_shape=[jax.ShapeDtypeStruct((db, H_C, DV_C), F32),
                   jax.ShapeDtypeStruct((db, H_C, DK_C, DV_C), F32)],
        compiler_params=_cparams(("arbitrary",)),
        name="gla_sample",
    )(q_s, k_s, v_s, r_s, gt_s, state, wg2p, b_gate.reshape(1, -1), b_r.reshape(1, -1), gn.reshape(1, -1))


def _swa_head_perm():
    perm = np.zeros((H_B * HD_B,), np.int32)
    for p in range(G_B):
        for kv in range(KV_B):
            for d in range(HD_B):
                perm[p * LANE + kv * HD_B + d] = (kv * G_B + p) * HD_B + d
    return perm


def _sample_rows(hm, np_rows, db):
    return jnp.transpose(hm[:, np_rows:np_rows + db, :], (1, 0, 2)).astype(F32)


def kernel(x_prompt, x_sample, cache_k_a, cache_v_a, state_swa_k, state_swa_v, state_gla, page_table,
           w_qkv_a, lam_a, subln_a, w_o_a, w_qkv_b, b_qkv_b, sinks_b, w_o_b, b_o_b,
           w_in_c, w_gate2_c, b_gate_c, b_r_c, gn_c, w_o_c, ln1_g, ln1_b, ln2_g, ln2_b,
           w_gu_d, w_down_d, w_router, w_gu_e, w_down_e):
    batch, t_len, _ = x_prompt.shape
    db = x_sample.shape[0]
    assert x_sample.shape[1] == 1 and db <= SAMPLE_PAD
    n_p = batch * t_len
    nt = n_p + SAMPLE_PAD
    n_real = n_p + db
    assert nt % TOK_TILE == 0

    y = jnp.concatenate([x_prompt.reshape(n_p, D_MODEL), x_sample.reshape(db, D_MODEL),
                         jnp.zeros((SAMPLE_PAD - db, D_MODEL), F32)], axis=0)
    yb = y.astype(BF16)
    cache_k = cache_k_a.reshape(cache_k_a.shape[:2] + (PAGE_SIZE * KV_A, LANE))
    cache_v = cache_v_a.reshape(cache_v_a.shape[:2] + (PAGE_SIZE * KV_A, LANE))
    w_gu_e2 = w_gu_e.reshape((-1,) + w_gu_e.shape[2:])
    w_down_e2 = w_down_e.reshape((-1,) + w_down_e.shape[2:])
    zero_bias = jnp.zeros((D_MODEL,), F32)
    perm = _swa_head_perm()
    ffn_tile = FFN_TILE if nt % FFN_TILE == 0 else TOK_TILE
    n_ffn_tiles = nt // ffn_tile

    ka_p, va_p, ka_s, va_s = [], [], [], []
    kb_p, vb_p, kb_s, vb_s = [], [], [], []
    gc_p, gc_s = [], []
    for i in range(DEPTH):
        j = i // 3
        if i % 3 == 0:
            nq = H_A * LANE
            (q_hm,) = _proj(yb, w_qkv_a, jnp.zeros((nq,), F32), layer=j, col0=0, ncols=nq, cw=512,
                            hm_dtype=BF16, emit_tok=None)
            kv_hm, k_tok, v_tok = _proj(yb, w_qkv_a, jnp.zeros((nq,), F32), layer=j, col0=nq, ncols=nq, cw=nq,
                                        hm_dtype=BF16, emit_tok="3d")
            ka_p.append(k_tok[:n_p].reshape(batch, t_len, KV_A, LANE))
            va_p.append(v_tok[:n_p].reshape(batch, t_len, KV_A, LANE))
            k_new = k_tok[n_p:n_real]
            v_new = v_tok[n_p:n_real]
            ka_s.append(k_new.reshape(db, 1, KV_A, LANE))
            va_s.append(v_new.reshape(db, 1, KV_A, LANE))
            lam_full, lam_init = _diff_lambda(lam_a[j], i)
            attn_p = _diffattn_prompt(q_hm, kv_hm, lam_full, lam_init, subln_a[j], batch, t_len)
            attn_s = _diffattn_sample(_sample_rows(q_hm, n_p, db), k_new, v_new, cache_k, cache_v, j,
                                      page_table, lam_full, lam_init, subln_a[j]).reshape(db, H_A * LANE)
            w_o, lay_o, b_o = w_o_a, j, zero_bias
        elif i % 3 == 1:
            nq = H_B * HD_B
            w_q = w_qkv_b[j][:, :nq][:, perm][None]
            b_q = b_qkv_b[j][:nq][perm]
            w_kv = w_qkv_b[j][:, nq:][None]
            b_kv = b_qkv_b[j][nq:]
            (q_hm,) = _proj(yb, w_q, b_q, layer=0, col0=0, ncols=nq, cw=512, hm_dtype=BF16, emit_tok=None)
            kv_hm, kv_tok = _proj(yb, w_kv, b_kv, layer=0, col0=0, ncols=2 * LANE, cw=2 * LANE,
                                  hm_dtype=BF16, emit_tok="2d")
            k_tok = kv_tok[:, :LANE]
            v_tok = kv_tok[:, LANE:]
            last_w = lambda a: a[:n_p].reshape(batch, t_len, LANE)[:, t_len - WINDOW:].reshape(
                batch, WINDOW, KV_B, HD_B)
            kb_p.append(last_w(k_tok))
            vb_p.append(last_w(v_tok))
            k_new = k_tok[n_p:n_real].reshape(db, 1, LANE)
            v_new = v_tok[n_p:n_real].reshape(db, 1, LANE)
            k_buf = state_swa_k[j].reshape(db, WINDOW, LANE)
            v_buf = state_swa_v[j].reshape(db, WINDOW, LANE)
            kb_s.append(jnp.concatenate([k_buf[:, 1:], k_new], axis=1).reshape(db, WINDOW, KV_B, HD_B))
            vb_s.append(jnp.concatenate([v_buf[:, 1:], v_new], axis=1).reshape(db, WINDOW, KV_B, HD_B))
            attn_p = _swa_prompt(q_hm, kv_hm, sinks_b[j], batch, t_len)
            attn_s = _swa_sample(_sample_rows(q_hm, n_p, db), k_buf, v_buf, k_new, v_new,
                                 sinks_b[j]).reshape(db, G_B * LANE)
            w_o, lay_o, b_o = w_o_b[j][perm][None], 0, b_o_b[j]
        else:
            n_in = w_in_c.shape[2]
            n_main = 2 * H_C * DK_C + 2 * H_C * DV_C
            w_in = jnp.zeros((1, D_MODEL, n_main + LANE), F32).at[0, :, :n_in].set(w_in_c[j])
            (hm,) = _proj(yb, w_in, jnp.zeros((n_main + LANE,), F32), layer=0, col0=0,
                          ncols=n_main + LANE, cw=5 * LANE, hm_dtype=F32, emit_tok=None)
            wg2p = jnp.zeros((LANE, H_C * DK_C), F32).at[:GATE_RANK].set(w_gate2_c[j])
            attn_p, st_p = _gla_prompt(hm, wg2p, b_gate_c[j], b_r_c[j], gn_c[j], batch, t_len)
            gc_p.append(st_p)
            hs = _sample_rows(hm, n_p, db)
            q_s = hs[:, 0:H_C]
            k_s = hs[:, H_C:2 * H_C]
            v_s = hs[:, 2 * H_C:4 * H_C].reshape(db, H_C, DV_C)
            r_s = hs[:, 4 * H_C:6 * H_C].reshape(db, H_C, DV_C)
            gt_s = hs[:, 6 * H_C:6 * H_C + 1]
            attn_s, st_s = _gla_sample(q_s, k_s, v_s, r_s, gt_s, state_gla[j].astype(F32), wg2p,
                                       b_gate_c[j], b_r_c[j], gn_c[j])
            gc_s.append(st_s)
            attn_s = attn_s.reshape(db, H_C * DV_C)
            w_o, lay_o, b_o = w_o_c, j, zero_bias

        f = i // 2
        if i % 2 == 0:
            y1, y1b = _outproj_ln(attn_p, attn_s, w_o, lay_o, b_o, y, ln1_g[i], ln1_b[i])
            y, yb = _ffn(y1b, w_gu_d, w_down_d, jnp.full((n_ffn_tiles,), f, jnp.int32),
                         jnp.ones((n_ffn_tiles,), jnp.int32), tm=ffn_tile,
                         dense_args=(y1, ln2_g[i], ln2_b[i]))
        else:
            y1, _, route = _outproj_ln(attn_p, attn_s, w_o, lay_o, b_o, y, ln1_g[i], ln1_b[i], w_router[f])
            y, yb = _moe(y1, route, n_real, w_gu_e2, w_down_e2, f * N_EXPERTS, ln2_g[i], ln2_b[i])

    yp = y[:n_p].reshape(batch, t_len, D_MODEL)
    ys = y[n_p:n_real].reshape(db, 1, D_MODEL)
    return (yp, ys, jnp.stack(ka_p), jnp.stack(va_p), jnp.stack(ka_s), jnp.stack(va_s),
            jnp.stack(kb_p), jnp.stack(vb_p), jnp.stack(kb_s), jnp.stack(vb_s),
            jnp.stack(gc_p), jnp.stack(gc_s))
```

```python
import functools
import math

import jax
import jax.numpy as jnp
import numpy as np
from jax import lax
from jax.experimental import pallas as pl
from jax.experimental.pallas import tpu as pltpu

F32 = jnp.float32
BF16 = jnp.bfloat16

D_MODEL = 1024
DEPTH = 4
PAGE_SIZE = 128
H_A, KV_A, G_A, HD_A = 8, 4, 2, 64
H_B, KV_B, G_B, HD_B = 16, 2, 8, 64
WINDOW = 128
H_C, DK_C, DV_C = 4, 128, 256
GATE_RANK = 16
GATE_TAU = 16.0
D_FF = 2816
N_EXPERTS = 8
LN_EPS = 1e-5
RMS_EPS = 1e-6
ALPHA = (2 * DEPTH) ** 0.25

LANE = 128
VMEM_LIMIT = 56 * 1024 * 1024
NEG_INF = float("-inf")

SAMPLE_PAD = 512
TOK_TILE = 768
FF_TILE = 256
MOE_TILE = 1536
FFN_TILE = 1056
Q_TILE = 512
K_TILE = 512
GLA_SUB = 64
GLA_TILE = 256
PAGES_PER_STEP = 16


def _cparams(sem):
    return pltpu.CompilerParams(dimension_semantics=sem, vmem_limit_bytes=VMEM_LIMIT)


def _layer_norm_rows(x, g, b):
    mu = jnp.mean(x, axis=-1, keepdims=True)
    xc = x - mu
    var = jnp.mean(xc * xc, axis=-1, keepdims=True)
    return xc * lax.rsqrt(var + LN_EPS) * g + b


def _proj_kernel(x_ref, w_ref, b_ref, *rest, emit_tok):
    hm_ref, wb_ref = rest[0], rest[-1]
    tok_refs = rest[1:-1]

    @pl.when(pl.program_id(1) == 0)
    def _():
        wb_ref[...] = w_ref[...].astype(BF16)

    y = jnp.dot(x_ref[...], wb_ref[...], preferred_element_type=F32) + b_ref[...]
    for c in range(hm_ref.shape[0]):
        hm_ref[c] = y[:, c * LANE:(c + 1) * LANE].astype(hm_ref.dtype)
        if emit_tok == "3d":
            per = tok_refs[0].shape[1]
            tok_refs[c // per][:, c % per, :] = y[:, c * LANE:(c + 1) * LANE]
    if emit_tok == "2d":
        tok_refs[0][...] = y


def _proj(xb, w, b, *, layer, col0, ncols, cw, hm_dtype, emit_tok, tm=TOK_TILE):
    nt = xb.shape[0]
    assert ncols % cw == 0 and col0 % cw == 0 and cw % LANE == 0 and nt % tm == 0
    nb = cw // LANE
    j0 = col0 // cw
    out_shape = [jax.ShapeDtypeStruct((ncols // LANE, nt, LANE), hm_dtype)]
    out_specs = [pl.BlockSpec((nb, tm, LANE), lambda j, i: (j, i, 0))]
    if emit_tok == "3d":
        assert cw == ncols and nb % KV_A == 0
        for _ in range(nb // KV_A):
            out_shape.append(jax.ShapeDtypeStruct((nt, KV_A, LANE), F32))
            out_specs.append(pl.BlockSpec((tm, KV_A, LANE), lambda j, i: (i, 0, 0)))
    elif emit_tok == "2d":
        out_shape.append(jax.ShapeDtypeStruct((nt, ncols), F32))
        out_specs.append(pl.BlockSpec((tm, cw), lambda j, i: (i, j)))
    return pl.pallas_call(
        functools.partial(_proj_kernel, emit_tok=emit_tok),
        grid=(ncols // cw, nt // tm),
        in_specs=[pl.BlockSpec((tm, D_MODEL), lambda j, i: (i, 0)),
                  pl.BlockSpec((None, D_MODEL, cw), lambda j, i: (layer, 0, j0 + j)),
                  pl.BlockSpec((1, cw), lambda j, i: (0, j))],
        out_specs=out_specs,
        out_shape=out_shape,
        scratch_shapes=[pltpu.VMEM((D_MODEL, cw), BF16)],
        compiler_params=_cparams(("arbitrary", "arbitrary")),
        name="proj",
    )(xb, w, b.reshape(1, ncols))


def _top2_route(logits):
    lane = lax.broadcasted_iota(jnp.int32, logits.shape, 1)
    l1 = jnp.where(lane < N_EXPERTS, logits, NEG_INF)
    m1 = jnp.max(l1, axis=-1, keepdims=True)
    i1 = jnp.min(jnp.where(l1 == m1, lane, LANE), axis=-1, keepdims=True)
    l2 = jnp.where(lane == i1, NEG_INF, l1)
    m2 = jnp.max(l2, axis=-1, keepdims=True)
    i2 = jnp.min(jnp.where(l2 == m2, lane, LANE), axis=-1, keepdims=True)
    e = jnp.exp(m2 - m1)
    g1 = 1.0 / (1.0 + e)
    g2 = e / (1.0 + e)
    return jnp.where(lane == 0, g1,
                     jnp.where(lane == 1, g2,
                               jnp.where(lane == 2, i1.astype(F32),
                                         jnp.where(lane == 3, i2.astype(F32), 0.0))))


def _outproj_kernel(ap_ref, as_ref, w_ref, b_ref, res_ref, g_ref, be_ref, *rest, route):
    if route:
        wr_ref, y_ref, yb_ref, rt_ref, wb_ref = rest
    else:
        y_ref, yb_ref, wb_ref = rest

    @pl.when(pl.program_id(0) == 0)
    def _():
        wb_ref[...] = w_ref[...].astype(BF16)

    a = jnp.where(pl.program_id(0) == pl.num_programs(0) - 1, as_ref[...], ap_ref[...])
    m = jnp.dot(a, wb_ref[...], preferred_element_type=F32) + b_ref[...]
    y = _layer_norm_rows(ALPHA * res_ref[...] + m, g_ref[...], be_ref[...])
    y_ref[...] = y
    yb_ref[...] = y.astype(BF16)
    if route:
        logits = jnp.dot(y, wr_ref[...], preferred_element_type=F32, precision=lax.Precision.HIGHEST)
        rt_ref[...] = _top2_route(logits)


def _outproj_ln(a_p, a_s, w, layer, b, res, g, be, w_router=None):
    tm = SAMPLE_PAD
    n_p, k = a_p.shape
    nt = res.shape[0]
    assert n_p % tm == 0 and nt == n_p + tm
    a_s = jnp.concatenate([a_s.astype(BF16), jnp.zeros((tm - a_s.shape[0], k), BF16)], axis=0)
    n_ptiles = n_p // tm
    route = w_router is not None
    row = lambda i: (i, 0)
    const = lambda i: (0, 0)
    in_specs = [pl.BlockSpec((tm, k), lambda i: (jnp.minimum(i, n_ptiles - 1), 0)),
                pl.BlockSpec((tm, k), const), pl.BlockSpec((None, k, D_MODEL), lambda i: (layer, 0, 0)),
                pl.BlockSpec((1, D_MODEL), const), pl.BlockSpec((tm, D_MODEL), row),
                pl.BlockSpec((1, D_MODEL), const), pl.BlockSpec((1, D_MODEL), const)]
    args = [a_p, a_s, w, b.reshape(1, D_MODEL), res, g.reshape(1, D_MODEL), be.reshape(1, D_MODEL)]
    out_shape = [jax.ShapeDtypeStruct((nt, D_MODEL), F32), jax.ShapeDtypeStruct((nt, D_MODEL), BF16)]
    out_specs = [pl.BlockSpec((tm, D_MODEL), row), pl.BlockSpec((tm, D_MODEL), row)]
    if route:
        wr = jnp.zeros((D_MODEL, LANE), F32).at[:, :N_EXPERTS].set(w_router)
        in_specs.append(pl.BlockSpec((D_MODEL, LANE), const))
        args.append(wr)
        out_shape.append(jax.ShapeDtypeStruct((nt, LANE), F32))
        out_specs.append(pl.BlockSpec((tm, LANE), row))
    return pl.pallas_call(
        functools.partial(_outproj_kernel, route=route),
        grid=(nt // tm,),
        in_specs=in_specs, out_specs=out_specs, out_shape=out_shape,
        scratch_shapes=[pltpu.VMEM((k, D_MODEL), BF16)],
        compiler_params=_cparams(("arbitrary",)),
        name="outproj_ln",
    )(*args)


def _ffn_kernel(te_ref, tv_ref, x_ref, wg_ref, wu_ref, wd_ref, *rest, dense):
    if dense:
        res_ref, g_ref, be_ref, y_ref, yb_ref, acc_ref = rest
    else:
        y_ref, xb_ref = rest
        acc_ref = y_ref
    i = pl.program_id(0)
    j = pl.program_id(1)
    nj = pl.num_programs(1)

    @pl.when(j == 0)
    def _():
        acc_ref[...] = jnp.zeros(acc_ref.shape, F32)

    @pl.when(tv_ref[i] > 0)
    def _():
        if dense:
            x = x_ref[...]
        else:
            @pl.when(j == 0)
            def _():
                xb_ref[...] = x_ref[...].astype(BF16)
            x = xb_ref[...]
        gate = jnp.dot(x, wg_ref[...].astype(BF16), preferred_element_type=F32)
        up = jnp.dot(x, wu_ref[...].astype(BF16), preferred_element_type=F32)
        h = (gate * (1.0 / (1.0 + jnp.exp(-gate))) * up).astype(BF16)
        acc_ref[...] += jnp.dot(h, wd_ref[...].astype(BF16), preferred_element_type=F32)

        if dense:
            @pl.when(j == nj - 1)
            def _():
                y = _layer_norm_rows(ALPHA * res_ref[...] + acc_ref[...], g_ref[...], be_ref[...])
                y_ref[...] = y
                yb_ref[...] = y.astype(BF16)


def _ffn(xb, w_gu, w_down, tile_expert, tile_valid, *, tm, dense_args=None, tf=FF_TILE):
    m = xb.shape[0]
    assert m % tm == 0 and D_FF % tf == 0
    nf = D_FF // tf
    dense = dense_args is not None

    def jeff(i, j, tv):
        return jnp.where(tv[i] > 0, j, nf - 1)

    in_specs = [
        pl.BlockSpec((tm, xb.shape[1]), lambda i, j, te, tv: (i, 0)),
        pl.BlockSpec((None, D_MODEL, tf), lambda i, j, te, tv: (te[i], 0, jeff(i, j, tv))),
        pl.BlockSpec((None, D_MODEL, tf), lambda i, j, te, tv: (te[i], 0, nf + jeff(i, j, tv))),
        pl.BlockSpec((None, tf, D_MODEL), lambda i, j, te, tv: (te[i], jeff(i, j, tv), 0)),
    ]
    args = [xb, w_gu, w_gu, w_down]
    row = lambda i, j, te, tv: (i, 0)
    const = lambda i, j, te, tv: (0, 0)
    if dense:
        res, g, be = dense_args
        in_specs += [pl.BlockSpec((tm, D_MODEL), row), pl.BlockSpec((1, D_MODEL), const),
                     pl.BlockSpec((1, D_MODEL), const)]
        args += [res, g.reshape(1, D_MODEL), be.reshape(1, D_MODEL)]
        out_shape = [jax.ShapeDtypeStruct((m, D_MODEL), F32), jax.ShapeDtypeStruct((m, D_MODEL), BF16)]
        out_specs = [pl.BlockSpec((tm, D_MODEL), row), pl.BlockSpec((tm, D_MODEL), row)]
    else:
        out_shape = jax.ShapeDtypeStruct((m, D_MODEL), F32)
        out_specs = pl.BlockSpec((tm, D_MODEL), row)
    scratch = [pltpu.VMEM((tm, D_MODEL), F32) if dense else pltpu.VMEM((tm, D_MODEL), BF16)]
    return pl.pallas_call(
        functools.partial(_ffn_kernel, dense=dense),
        grid_spec=pltpu.PrefetchScalarGridSpec(
            num_scalar_prefetch=2, grid=(m // tm, nf),
            in_specs=in_specs, out_specs=out_specs, scratch_shapes=scratch),
        out_shape=out_shape,
        compiler_params=_cparams(("arbitrary", "arbitrary")),
        name="ffn_dense" if dense else "ffn_moe",
    )(tile_expert, tile_valid, *args)


def _combine_kernel(ya_ref, yb_ref, rt_ref, res_ref, g_ref, be_ref, y_ref, ybf_ref):
    rt = rt_ref[...]
    f = rt[:, 0:1] * ya_ref[...] + rt[:, 1:2] * yb_ref[...]
    y = _layer_norm_rows(ALPHA * res_ref[...] + f, g_ref[...], be_ref[...])
    y_ref[...] = y
    ybf_ref[...] = y.astype(BF16)


def _combine_ln(ya, yb, route, res, g, be, *, tm=TOK_TILE):
    nt = res.shape[0]
    row = lambda i: (i, 0)
    const = lambda i: (0, 0)
    return pl.pallas_call(
        _combine_kernel,
        grid=(nt // tm,),
        in_specs=[pl.BlockSpec((tm, D_MODEL), row), pl.BlockSpec((tm, D_MODEL), row),
                  pl.BlockSpec((tm, LANE), row), pl.BlockSpec((tm, D_MODEL), row),
                  pl.BlockSpec((1, D_MODEL), const), pl.BlockSpec((1, D_MODEL), const)],
        out_specs=[pl.BlockSpec((tm, D_MODEL), row), pl.BlockSpec((tm, D_MODEL), row)],
        out_shape=[jax.ShapeDtypeStruct((nt, D_MODEL), F32), jax.ShapeDtypeStruct((nt, D_MODEL), BF16)],
        compiler_params=_cparams(("arbitrary",)),
        name="moe_combine_ln",
    )(ya, yb, route, res, g.reshape(1, D_MODEL), be.reshape(1, D_MODEL))


def _moe(res, route, n_real, w_gu, w_down, e_off, g, be, *, tm=MOE_TILE):
    nt = res.shape[0]
    gates_idx = route[:n_real, 2:4].astype(jnp.int32)
    flat_e = gates_idx.reshape(-1)
    npair = flat_e.shape[0]
    onehot = (flat_e[:, None] == jnp.arange(N_EXPERTS)[None, :]).astype(jnp.int32)
    csum = jnp.cumsum(onehot, axis=0)
    counts = csum[-1]
    rank = jnp.take_along_axis(csum, flat_e[:, None], axis=1)[:, 0] - 1
    tiles_per = (counts + tm - 1) // tm
    tile_end = jnp.cumsum(tiles_per)
    tile_start = tile_end - tiles_per
    n_tiles = npair // tm + N_EXPERTS
    m_pad = n_tiles * tm
    dest = tile_start[flat_e] * tm + rank
    row_src = jnp.zeros((m_pad,), jnp.int32).at[dest].set(
        jnp.arange(npair, dtype=jnp.int32) // 2, unique_indices=True, mode="promise_in_bounds")
    t_ids = jnp.arange(n_tiles, dtype=jnp.int32)
    tile_valid = (t_ids < tile_end[-1]).astype(jnp.int32)
    tile_expert = jnp.sum((t_ids[:, None] >= tile_end[None, :]).astype(jnp.int32), axis=1)
    tile_expert = jnp.minimum(tile_expert, N_EXPERTS - 1)
    last_e = tile_expert[jnp.maximum(tile_end[-1] - 1, 0)]
    tile_expert = jnp.where(tile_valid > 0, tile_expert, last_e) + e_off
    x_sorted = res.at[row_src].get(mode="promise_in_bounds")
    y_sorted = _ffn(x_sorted, w_gu, w_down, tile_expert, tile_valid, tm=tm)
    dest2 = jnp.zeros((nt, 2), jnp.int32).at[:n_real].set(dest.reshape(n_real, 2))
    ya = y_sorted.at[dest2[:, 0]].get(mode="promise_in_bounds")
    ybb = y_sorted.at[dest2[:, 1]].get(mode="promise_in_bounds")
    return _combine_ln(ya, ybb, route, res, g, be)


def _diff_lambda(lam, layer_idx):
    lam_init = 0.8 - 0.6 * math.exp(-0.3 * layer_idx)
    lf = lam.astype(F32)
    lam_full = jnp.exp(jnp.sum(lf[0] * lf[1])) - jnp.exp(jnp.sum(lf[2] * lf[3])) + lam_init
    return lam_full, lam_init


def _alibi_slopes(n):
    return [2.0 ** (-8.0 * (i + 1) / n) for i in range(n)]


def _sub_rms(o, subln, out_scale):
    return o * lax.rsqrt(jnp.mean(o * o, axis=-1, keepdims=True) + RMS_EPS) * subln * out_scale


def _diffattn_prompt_kernel(q_ref, k_ref, v_ref, pos_ref, slope_ref, lam_ref, subln_ref, o_ref,
                            kaug_ref, vt_ref, qaug_ref, m_ref, l_ref, acc_ref, *, out_scale):
    qi = pl.program_id(2)
    tq = q_ref.shape[1]
    t_len = k_ref.shape[0]
    nrow = 2 * G_A * tq

    @pl.when(qi == 0)
    def _():
        kaug_ref[:, :LANE] = k_ref[...]
        kaug_ref[:, LANE:] = pos_ref[...]
        for jb in range(t_len // K_TILE):
            vt_ref[jb] = v_ref[jb * K_TILE:(jb + 1) * K_TILE, :].astype(F32).T.astype(BF16)

    lane = lax.broadcasted_iota(jnp.int32, (tq, LANE), 1)
    scale = HD_A ** -0.5
    for g in range(G_A):
        qg = q_ref[g] * scale
        qaug_ref[(2 * g) * tq:(2 * g + 1) * tq, :LANE] = jnp.where(lane < HD_A, qg, 0).astype(BF16)
        qaug_ref[(2 * g + 1) * tq:(2 * g + 2) * tq, :LANE] = jnp.where(lane >= HD_A, qg, 0).astype(BF16)
    qaug_ref[:, LANE:] = slope_ref[...]

    m_ref[...] = jnp.full(m_ref.shape, NEG_INF, F32)
    l_ref[...] = jnp.zeros(l_ref.shape, F32)
    acc_ref[...] = jnp.zeros(acc_ref.shape, F32)

    def block(j, masked):
        start = pl.multiple_of(j * K_TILE, K_TILE)
        kb = kaug_ref[pl.ds(start, K_TILE), :]
        s = lax.dot_general(kb, qaug_ref[...], (((1,), (1,)), ((), ())), preferred_element_type=F32)
        if masked:
            key = lax.broadcasted_iota(jnp.int32, (K_TILE, nrow), 0)
            col = lax.broadcasted_iota(jnp.int32, (K_TILE, nrow), 1)
            s = jnp.where(start + key <= qi * tq + (col & (tq - 1)), s, NEG_INF)
        m_old = m_ref[...]
        m_new = jnp.maximum(m_old, jnp.max(s, axis=0, keepdims=True))
        alpha = jnp.exp(m_old - m_new)
        p = jnp.exp(s - m_new)
        l_ref[...] = alpha * l_ref[...] + jnp.sum(p, axis=0, keepdims=True)
        acc_ref[...] = alpha * acc_ref[...] + jnp.dot(vt_ref[j], p.astype(BF16), preferred_element_type=F32)
        m_ref[...] = m_new

    n_full = (qi * tq) // K_TILE

    def body(j, carry):
        block(j, False)
        return carry

    lax.fori_loop(0, n_full, body, 0)
    block(n_full, True)

    lam = lam_ref[...]
    norm = acc_ref[...] / l_ref[...]
    for g in range(G_A):
        o = norm[:, (2 * g) * tq:(2 * g + 1) * tq].T - lam * norm[:, (2 * g + 1) * tq:(2 * g + 2) * tq].T
        o_ref[:, g * LANE:(g + 1) * LANE] = _sub_rms(o, subln_ref[...], out_scale).astype(o_ref.dtype)


def _alibi_aug_consts(t_len, tq):
    pos = np.arange(t_len)
    kcols = np.zeros((t_len, LANE), np.float32)
    kcols[:, 0] = (pos // 64) * 64
    kcols[:, 1] = pos % 64
    slopes = _alibi_slopes(H_A)
    qcols = np.zeros((KV_A, 2 * G_A * tq, LANE), np.float32)
    for kv in range(KV_A):
        for g in range(G_A):
            qcols[kv, 2 * g * tq:(2 * g + 2) * tq, 0:2] = slopes[kv * G_A + g]
    return jnp.asarray(kcols, BF16), jnp.asarray(qcols, BF16)


def _diffattn_prompt(q_hm, kv_hm, lam_full, lam_init, subln, batch, t_len):
    tq = Q_TILE
    nq = t_len // tq
    assert t_len % K_TILE == 0
    kcols, qcols = _alibi_aug_consts(t_len, tq)
    nrow = 2 * G_A * tq
    lam_row = jnp.full((1, LANE), lam_full, F32)
    return pl.pallas_call(
        functools.partial(_diffattn_prompt_kernel, out_scale=1.0 - lam_init),
        grid=(batch, KV_A, nq),
        in_specs=[pl.BlockSpec((G_A, tq, LANE), lambda b, kv, qi: (kv, b * nq + qi, 0)),
                  pl.BlockSpec((None, t_len, LANE), lambda b, kv, qi: (kv, b, 0)),
                  pl.BlockSpec((None, t_len, LANE), lambda b, kv, qi: (KV_A + kv, b, 0)),
                  pl.BlockSpec((t_len, LANE), lambda b, kv, qi: (0, 0)),
                  pl.BlockSpec((None, nrow, LANE), lambda b, kv, qi: (kv, 0, 0)),
                  pl.BlockSpec((1, LANE), lambda b, kv, qi: (0, 0)),
                  pl.BlockSpec((1, LANE), lambda b, kv, qi: (0, 0))],
        out_specs=pl.BlockSpec((tq, G_A * LANE), lambda b, kv, qi: (b * nq + qi, kv)),
        out_shape=jax.ShapeDtypeStruct((batch * t_len, H_A * LANE), BF16),
        scratch_shapes=[pltpu.VMEM((t_len, 2 * LANE), BF16), pltpu.VMEM((t_len // K_TILE, LANE, K_TILE), BF16),
                        pltpu.VMEM((nrow, 2 * LANE), BF16),
                        pltpu.VMEM((1, nrow), F32), pltpu.VMEM((1, nrow), F32), pltpu.VMEM((LANE, nrow), F32)],
        compiler_params=_cparams(("arbitrary", "arbitrary", "arbitrary")),
        name="diffattn_prompt",
    )(q_hm, kv_hm, kv_hm, kcols, qcols, lam_row, subln.reshape(1, LANE))


def _diffattn_sample_kernel(pt_ref, q_ref, kn_ref, vn_ref, lam_ref, subln_ref, slope_ref, *rest,
                            out_scale, past_len):
    pp = PAGES_PER_STEP
    k_refs = rest[:pp]
    v_refs = rest[pp:2 * pp]
    o_ref, m_ref, l_ref, acc_ref = rest[2 * pp:]
    c = pl.program_id(1)
    nc = pl.num_programs(1)
    nrow = KV_A * 8
    ncol = pp * PAGE_SIZE * KV_A

    @pl.when(c == 0)
    def _():
        m_ref[...] = jnp.full(m_ref.shape, NEG_INF, F32)
        l_ref[...] = jnp.zeros(l_ref.shape, F32)
        acc_ref[...] = jnp.zeros(acc_ref.shape, F32)

    kc = jnp.concatenate([r[...].astype(BF16) for r in k_refs], axis=0)
    vc = jnp.concatenate([r[...].astype(BF16) for r in v_refs], axis=0)
    s = lax.dot_general(q_ref[...].astype(BF16), kc, (((1,), (1,)), ((), ())), preferred_element_type=F32)
    row = lax.broadcasted_iota(jnp.int32, (nrow, ncol), 0)
    col = lax.broadcasted_iota(jnp.int32, (nrow, ncol), 1)
    pos = c * (pp * PAGE_SIZE) + (col >> 2)
    dist = past_len - pos.astype(F32)
    s = jnp.where((col & (KV_A - 1)) == (row >> 3), s - slope_ref[...] * dist, NEG_INF)
    m_old = m_ref[...]
    m_new = jnp.maximum(m_old, jnp.max(s, axis=-1, keepdims=True))
    alpha = jnp.exp(m_old - m_new)
    p = jnp.exp(s - m_new)
    l_ref[...] = alpha * l_ref[...] + jnp.sum(p, axis=-1, keepdims=True)
    acc_ref[...] = alpha * acc_ref[...] + jnp.dot(p.astype(BF16), vc, preferred_element_type=F32)
    m_ref[...] = m_new

    @pl.when(c == nc - 1)
    def _():
        lam = lam_ref[...]
        s_self = jnp.sum(q_ref[...] * kn_ref[...], axis=-1, keepdims=True)
        m_old = m_ref[...]
        m_fin = jnp.maximum(m_old, s_self)
        alpha = jnp.exp(m_old - m_fin)
        p_self = jnp.exp(s_self - m_fin)
        norm = (alpha * acc_ref[...] + p_self * vn_ref[...]) / (alpha * l_ref[...] + p_self)
        for kv in range(KV_A):
            for g in range(G_A):
                o = norm[kv * 8 + g:kv * 8 + g + 1] - lam * norm[kv * 8 + 4 + g:kv * 8 + 5 + g]
                o_ref[kv * G_A + g:kv * G_A + g + 1, :] = _sub_rms(o, subln_ref[...], out_scale)


def _diffattn_sample(q_s, k_new, v_new, cache_k, cache_v, layer_j, page_table, lam_full, lam_init, subln):
    db, n_pages = page_table.shape
    nrow = KV_A * 8
    qg = q_s.reshape(db, KV_A, G_A, LANE) * (HD_A ** -0.5)
    lo = jnp.arange(LANE) < HD_A
    zero = jnp.zeros((db, KV_A, 2, LANE), F32)
    q32 = jnp.concatenate([jnp.where(lo, qg, 0.0), zero, jnp.where(lo, 0.0, qg), zero], axis=2)
    q32 = q32.reshape(db, nrow, LANE)
    kn32 = jnp.repeat(k_new, 8, axis=1)
    vn32 = jnp.repeat(v_new, 8, axis=1)
    pp = PAGES_PER_STEP
    assert n_pages % pp == 0 and KV_A == 4
    past_len = n_pages * PAGE_SIZE
    slopes = np.zeros((nrow, 1), np.float32)
    sl = _alibi_slopes(H_A)
    for kv in range(KV_A):
        for g in range(G_A):
            slopes[kv * 8 + g, 0] = sl[kv * G_A + g]
            slopes[kv * 8 + 4 + g, 0] = sl[kv * G_A + g]
    lam_row = jnp.full((1, LANE), lam_full, F32)

    def page_spec(i):
        return pl.BlockSpec((None, None, PAGE_SIZE * KV_A, LANE),
                            lambda b, c, pt: (layer_j, pt[b, c * pp + i], 0, 0))

    vec = pl.BlockSpec((None, nrow, LANE), lambda b, c, pt: (b, 0, 0))
    const2 = lambda shape: pl.BlockSpec(shape, lambda b, c, pt: (0, 0))
    in_specs = ([vec, vec, vec, const2((1, LANE)), const2((1, LANE)), const2((nrow, 1))]
                + [page_spec(i) for i in range(pp)] * 2)
    return pl.pallas_call(
        functools.partial(_diffattn_sample_kernel, out_scale=1.0 - lam_init, past_len=float(past_len)),
        grid_spec=pltpu.PrefetchScalarGridSpec(
            num_scalar_prefetch=1, grid=(db, n_pages // pp),
            in_specs=in_specs,
            out_specs=pl.BlockSpec((None, H_A, LANE), lambda b, c, pt: (b, 0, 0)),
            scratch_shapes=[pltpu.VMEM((nrow, 1), F32), pltpu.VMEM((nrow, 1), F32),
                            pltpu.VMEM((nrow, LANE), F32)]),
        out_shape=jax.ShapeDtypeStruct((db, H_A, LANE), F32),
        compiler_params=_cparams(("arbitrary", "arbitrary")),
        name="diffattn_sample",
    )(page_table, q32, kn32, vn32, lam_row, subln.reshape(1, LANE), jnp.asarray(slopes),
      *([cache_k] * pp), *([cache_v] * pp))


def _swa_prompt_kernel(sink_ref, q_ref, kp_ref, kc_ref, vp_ref, vc_ref, o_ref):
    n = pl.program_id(1)
    w = WINDOW
    kband = jnp.concatenate([kp_ref[...], kc_ref[...]], axis=0)
    vband = jnp.concatenate([vp_ref[...], vc_ref[...]], axis=0)
    lane = lax.broadcasted_iota(jnp.int32, (w, LANE), 1)
    row = lax.broadcasted_iota(jnp.int32, (w, 2 * w), 0)
    col = lax.broadcasted_iota(jnp.int32, (w, 2 * w), 1)
    rel = (w + row) - col
    valid = jnp.logical_and(jnp.logical_and(rel >= 0, rel <= w), jnp.logical_or(col >= w, n > 0))
    relf = rel.astype(F32)
    slopes = _alibi_slopes(H_B)
    scale = HD_B ** -0.5
    o_kv = []
    for kv in range(KV_B):
        keep = (lane < HD_B) if kv == 0 else (lane >= HD_B)
        qs = jnp.concatenate([jnp.where(keep, q_ref[p] * scale, 0).astype(BF16) for p in range(G_B)], axis=0)
        s_all = lax.dot_general(qs, kband, (((1,), (1,)), ((), ())), preferred_element_type=F32)
        ps = []
        for p in range(G_B):
            h = kv * G_B + p
            s = jnp.where(valid, s_all[p * w:(p + 1) * w] - slopes[h] * relf, NEG_INF)
            sink = sink_ref[h]
            m = jnp.maximum(jnp.max(s, axis=-1, keepdims=True), sink)
            e = jnp.exp(s - m)
            denom = jnp.sum(e, axis=-1, keepdims=True) + jnp.exp(sink - m)
            ps.append((e / denom).astype(BF16))
        o_kv.append(jnp.dot(jnp.concatenate(ps, axis=0), vband, preferred_element_type=F32))
    for p in range(G_B):
        o = jnp.where(lane < HD_B, o_kv[0][p * w:(p + 1) * w], o_kv[1][p * w:(p + 1) * w])
        o_ref[:, p * LANE:(p + 1) * LANE] = o.astype(o_ref.dtype)


def _swa_prompt(q_hm, kv_hm, sinks, batch, t_len):
    w = WINDOW
    nb = t_len // w
    cur = lambda b, n, s: (b * nb + n, 0)
    prev = lambda b, n, s: (b * nb + jnp.maximum(n - 1, 0), 0)
    return pl.pallas_call(
        _swa_prompt_kernel,
        grid_spec=pltpu.PrefetchScalarGridSpec(
            num_scalar_prefetch=1, grid=(batch, nb),
            in_specs=[pl.BlockSpec((G_B, w, LANE), lambda b, n, s: (0,) + cur(b, n, s)),
                      pl.BlockSpec((None, w, LANE), lambda b, n, s: (0,) + prev(b, n, s)),
                      pl.BlockSpec((None, w, LANE), lambda b, n, s: (0,) + cur(b, n, s)),
                      pl.BlockSpec((None, w, LANE), lambda b, n, s: (1,) + prev(b, n, s)),
                      pl.BlockSpec((None, w, LANE), lambda b, n, s: (1,) + cur(b, n, s))],
            out_specs=pl.BlockSpec((w, G_B * LANE), cur)),
        out_shape=jax.ShapeDtypeStruct((batch * t_len, G_B * LANE), BF16),
        compiler_params=_cparams(("arbitrary", "arbitrary")),
        name="swa_prompt",
    )(sinks.astype(F32), q_hm, kv_hm, kv_hm, kv_hm, kv_hm)


def _swa_sample_kernel(q_ref, kb_ref, vb_ref, kn_ref, vn_ref, slope_ref, sink_ref, o_ref):
    w = WINDOW
    q = q_ref[...]
    s = lax.dot_general(q.astype(BF16), kb_ref[...].astype(BF16), (((1,), (1,)), ((), ())),
                        preferred_element_type=F32)
    dist = (w - lax.broadcasted_iota(jnp.int32, (H_B, w), 1)).astype(F32)
    s = s - slope_ref[...] * dist
    s_self = jnp.sum(q * kn_ref[...], axis=-1, keepdims=True)
    sink = sink_ref[...]
    m = jnp.maximum(jnp.maximum(jnp.max(s, axis=-1, keepdims=True), s_self), sink)
    e = jnp.exp(s - m)
    e_self = jnp.exp(s_self - m)
    denom = jnp.sum(e, axis=-1, keepdims=True) + e_self + jnp.exp(sink - m)
    o = (jnp.dot(e.astype(BF16), vb_ref[...].astype(BF16), preferred_element_type=F32)
         + e_self * vn_ref[...]) / denom
    lane = lax.broadcasted_iota(jnp.int32, (G_B, LANE), 1)
    o_ref[...] = jnp.where(lane < HD_B, o[:G_B], o[G_B:])


def _swa_sample(q_s, k_buf, v_buf, k_new, v_new, sinks):
    db = q_s.shape[0]
    lo = jnp.arange(LANE) < HD_B
    qs = q_s * (HD_B ** -0.5)
    q16 = jnp.concatenate([jnp.where(lo, qs, 0.0), jnp.where(lo, 0.0, qs)], axis=1)
    slopes = jnp.asarray(np.asarray(_alibi_slopes(H_B), np.float32).reshape(H_B, 1))
    per_b3 = lambda shape: pl.BlockSpec(shape, lambda b: (b, 0, 0))
    const = pl.BlockSpec((H_B, 1), lambda b: (0, 0))
    return pl.pallas_call(
        _swa_sample_kernel,
        grid=(db,),
        in_specs=[per_b3((None, H_B, LANE)), per_b3((None, WINDOW, LANE)), per_b3((None, WINDOW, LANE)),
                  per_b3((None, 1, LANE)), per_b3((None, 1, LANE)), const, const],
        out_specs=per_b3((None, G_B, LANE)),
        out_shape=jax.ShapeDtypeStruct((db, G_B, LANE), F32),
        compiler_params=_cparams(("arbitrary",)),
        name="swa_sample",
    )(q16, k_buf, v_buf, k_new, v_new, slopes, sinks.astype(F32).reshape(H_B, 1))


def _log_sigmoid(z):
    return jnp.minimum(z, 0.0) - jnp.log(1.0 + jnp.exp(-jnp.abs(z)))


def _silu(x):
    return x * (1.0 / (1.0 + jnp.exp(-x)))


def _gla_prompt_kernel(q_ref, k_ref, v_ref, r_ref, gt_ref, wg2_ref, bg_ref, br_ref, gn_ref,
                       o_ref, st_ref, s_ref):
    ct = pl.program_id(1)
    c = GLA_SUB
    rows_per_step = gt_ref.shape[0]

    @pl.when(ct == 0)
    def _():
        s_ref[...] = jnp.zeros(s_ref.shape, F32)

    z = jnp.dot(gt_ref[...], wg2_ref[...], preferred_element_type=F32,
                precision=lax.Precision.HIGHEST) + bg_ref[...]
    g_all = _log_sigmoid(z) * (1.0 / GATE_TAU)
    ri = lax.broadcasted_iota(jnp.int32, (c, c), 0)
    ci = lax.broadcasted_iota(jnp.int32, (c, c), 1)
    causal = ri >= ci
    tri = causal.astype(F32)
    for sub in range(rows_per_step // c):
        r0 = sub * c
        for h in range(H_C):
            g = g_all[r0:r0 + c, h * DK_C:(h + 1) * DK_C]
            bc = jnp.dot(tri, g, preferred_element_type=F32, precision=lax.Precision.HIGHEST)
            q = q_ref[h, r0:r0 + c, :] * (DK_C ** -0.5)
            k = k_ref[h, r0:r0 + c, :]
            v = jnp.concatenate([v_ref[2 * h, r0:r0 + c, :], v_ref[2 * h + 1, r0:r0 + c, :]], axis=1)
            vb = v.astype(BF16)
            q_dec = (q * jnp.exp(bc)).astype(BF16)
            k_dec = (k * jnp.exp(-bc)).astype(BF16)
            att = lax.dot_general(q_dec, k_dec, (((1,), (1,)), ((), ())), preferred_element_type=F32)
            att = jnp.where(causal, att, 0.0).astype(BF16)
            state = s_ref[h]
            o = (jnp.dot(q_dec, state.astype(BF16), preferred_element_type=F32)
                 + jnp.dot(att, vb, preferred_element_type=F32))
            b_last = bc[c - 1:c, :]
            k_tail = (k * jnp.exp(b_last - bc)).astype(BF16)
            decay_col = jnp.transpose(jnp.broadcast_to(jnp.exp(b_last), (8, DK_C)))[:, 0:1]
            s_ref[h] = decay_col * state + lax.dot_general(
                k_tail, vb, (((0,), (0,)), ((), ())), preferred_element_type=F32)
            r = jnp.concatenate([r_ref[2 * h, r0:r0 + c, :], r_ref[2 * h + 1, r0:r0 + c, :]], axis=1)
            r = _silu(r + br_ref[:, h * DV_C:(h + 1) * DV_C])
            on = o * lax.rsqrt(jnp.mean(o * o, axis=-1, keepdims=True) + RMS_EPS) * gn_ref[...]
            o_ref[r0:r0 + c, h * DV_C:(h + 1) * DV_C] = (on * r).astype(o_ref.dtype)

    @pl.when(ct == pl.num_programs(1) - 1)
    def _():
        st_ref[...] = s_ref[...]


def _gla_prompt(hm, wg2p, b_gate, b_r, gn, batch, t_len):
    rt = GLA_TILE
    nc = t_len // rt
    rowblk = lambda b, ct: b * nc + ct
    const = lambda b, ct: (0, 0)
    return pl.pallas_call(
        _gla_prompt_kernel,
        grid=(batch, nc),
        in_specs=[pl.BlockSpec((H_C, rt, LANE), lambda b, ct: (0, rowblk(b, ct), 0)),
                  pl.BlockSpec((H_C, rt, LANE), lambda b, ct: (1, rowblk(b, ct), 0)),
                  pl.BlockSpec((2 * H_C, rt, LANE), lambda b, ct: (1, rowblk(b, ct), 0)),
                  pl.BlockSpec((2 * H_C, rt, LANE), lambda b, ct: (2, rowblk(b, ct), 0)),
                  pl.BlockSpec((None, rt, LANE), lambda b, ct: (6 * H_C, rowblk(b, ct), 0)),
                  pl.BlockSpec((LANE, H_C * DK_C), const), pl.BlockSpec((1, H_C * DK_C), const),
                  pl.BlockSpec((1, H_C * DV_C), const), pl.BlockSpec((1, DV_C), const)],
        out_specs=[pl.BlockSpec((rt, H_C * DV_C), lambda b, ct: (rowblk(b, ct), 0)),
                   pl.BlockSpec((None, H_C, DK_C, DV_C), lambda b, ct: (b, 0, 0, 0))],
        out_shape=[jax.ShapeDtypeStruct((batch * t_len, H_C * DV_C), BF16),
                   jax.ShapeDtypeStruct((batch, H_C, DK_C, DV_C), F32)],
        scratch_shapes=[pltpu.VMEM((H_C, DK_C, DV_C), F32)],
        compiler_params=_cparams(("arbitrary", "arbitrary")),
        name="gla_prompt",
    )(hm, hm, hm, hm, hm, wg2p, b_gate.reshape(1, -1), b_r.reshape(1, -1), gn.reshape(1, -1))


def _gla_sample_kernel(q_ref, k_ref, v_ref, r_ref, gt_ref, s_ref, wg2_ref, bg_ref, br_ref, gn_ref,
                       o_ref, so_ref):
    hi = lax.Precision.HIGHEST
    z = jnp.dot(jnp.broadcast_to(gt_ref[...], (8, LANE)), wg2_ref[...], preferred_element_type=F32,
                precision=hi)[0:1] + bg_ref[...]
    g_all = _log_sigmoid(z) * (1.0 / GATE_TAU)
    ri = lax.broadcasted_iota(jnp.int32, (DK_C, DK_C), 0)
    ci = lax.broadcasted_iota(jnp.int32, (DK_C, DK_C), 1)
    eye = ri == ci
    for h in range(H_C):
        eg = jnp.exp(g_all[:, h * DK_C:(h + 1) * DK_C])
        q = q_ref[h:h + 1, :] * (DK_C ** -0.5)
        k = k_ref[h:h + 1, :]
        v = v_ref[h:h + 1, :]
        state = s_ref[h]
        lhs = jnp.concatenate([jnp.where(eye, jnp.broadcast_to(eg, (DK_C, DK_C)), 0.0),
                               jnp.where(eye, jnp.broadcast_to(k, (DK_C, DK_C)), 0.0)], axis=1)
        rhs = jnp.concatenate([state, jnp.broadcast_to(v, (DK_C, DV_C))], axis=0)
        so_ref[h] = jnp.dot(lhs, rhs, preferred_element_type=F32, precision=hi)
        qd = q * eg
        o = (jnp.dot(jnp.broadcast_to(qd, (8, DK_C)).astype(BF16), state.astype(BF16),
                     preferred_element_type=F32)[0:1]
             + jnp.sum(q * k, axis=-1, keepdims=True) * v)
        r = _silu(r_ref[h:h + 1, :] + br_ref[:, h * DV_C:(h + 1) * DV_C])
        on = o * lax.rsqrt(jnp.mean(o * o, axis=-1, keepdims=True) + RMS_EPS) * gn_ref[...]
        o_ref[h:h + 1, :] = on * r


def _gla_sample(q_s, k_s, v_s, r_s, gt_s, state, wg2p, b_gate, b_r, gn):
    db = q_s.shape[0]
    b3 = lambda shape: pl.BlockSpec(shape, lambda b: (b, 0, 0))
    b4 = lambda shape: pl.BlockSpec(shape, lambda b: (b, 0, 0, 0))
    const = lambda shape: pl.BlockSpec(shape, lambda b: (0, 0))
    return pl.pallas_call(
        _gla_sample_kernel,
        grid=(db,),
        in_specs=[b3((None, H_C, DK_C)), b3((None, H_C, DK_C)), b3((None, H_C, DV_C)), b3((None, H_C, DV_C)),
                  b3((None, 1, LANE)), b4((None, H_C, DK_C, DV_C)),
                  const((LANE, H_C * DK_C)), const((1, H_C * DK_C)), const((1, H_C * DV_C)), const((1, DV_C))],
        out_specs=[b3((None, H_C, DV_C)), b4((None, H_C, DK_C, DV_C))],
        out_shape=[jax.ShapeDtypeStruct((db, H_C, DV_C), F32),
                   jax.ShapeDtypeStruct((db, H_C, DK_C, DV_C), F32)],
        compiler_params=_cparams(("arbitrary",)),
        name="gla_sample",
    )(q_s, k_s, v_s, r_s, gt_s, state, wg2p, b_gate.reshape(1, -1), b_r.reshape(1, -1), gn.reshape(1, -1))


def _swa_head_perm():
    perm = np.zeros((H_B * HD_B,), np.int32)
    for p in range(G_B):
        for kv in range(KV_B):
            for d in range(HD_B):
                perm[p * LANE + kv * HD_B + d] = (kv * G_B + p) * HD_B + d
    return perm


def _sample_rows(hm, np_rows, db):
    return jnp.transpose(hm[:, np_rows:np_rows + db, :], (1, 0, 2)).astype(F32)


def kernel(x_prompt, x_sample, cache_k_a, cache_v_a, state_swa_k, state_swa_v, state_gla, page_table,
           w_qkv_a, lam_a, subln_a, w_o_a, w_qkv_b, b_qkv_b, sinks_b, w_o_b, b_o_b,
           w_in_c, w_gate2_c, b_gate_c, b_r_c, gn_c, w_o_c, ln1_g, ln1_b, ln2_g, ln2_b,
           w_gu_d, w_down_d, w_router, w_gu_e, w_down_e):
    batch, t_len, _ = x_prompt.shape
    db = x_sample.shape[0]
    assert x_sample.shape[1] == 1 and db <= SAMPLE_PAD
    n_p = batch * t_len
    nt = n_p + SAMPLE_PAD
    n_real = n_p + db
    assert nt % TOK_TILE == 0

    y = jnp.concatenate([x_prompt.reshape(n_p, D_MODEL), x_sample.reshape(db, D_MODEL),
                         jnp.zeros((SAMPLE_PAD - db, D_MODEL), F32)], axis=0)
    yb = y.astype(BF16)
    cache_k = cache_k_a.reshape(cache_k_a.shape[:2] + (PAGE_SIZE * KV_A, LANE))
    cache_v = cache_v_a.reshape(cache_v_a.shape[:2] + (PAGE_SIZE * KV_A, LANE))
    w_gu_e2 = w_gu_e.reshape((-1,) + w_gu_e.shape[2:])
    w_down_e2 = w_down_e.reshape((-1,) + w_down_e.shape[2:])
    zero_bias = jnp.zeros((D_MODEL,), F32)
    perm = _swa_head_perm()
    ffn_tile = FFN_TILE if nt % FFN_TILE == 0 else TOK_TILE
    n_ffn_tiles = nt // ffn_tile

    ka_p, va_p, ka_s, va_s = [], [], [], []
    kb_p, vb_p, kb_s, vb_s = [], [], [], []
    gc_p, gc_s = [], []
    for i in range(DEPTH):
        j = i // 3
        if i % 3 == 0:
            nq = H_A * LANE
            (q_hm,) = _proj(yb, w_qkv_a, jnp.zeros((nq,), F32), layer=j, col0=0, ncols=nq, cw=512,
                            hm_dtype=BF16, emit_tok=None)
            kv_hm, k_tok, v_tok = _proj(yb, w_qkv_a, jnp.zeros((nq,), F32), layer=j, col0=nq, ncols=nq, cw=nq,
                                        hm_dtype=BF16, emit_tok="3d")
            ka_p.append(k_tok[:n_p].reshape(batch, t_len, KV_A, LANE))
            va_p.append(v_tok[:n_p].reshape(batch, t_len, KV_A, LANE))
            k_new = k_tok[n_p:n_real]
            v_new = v_tok[n_p:n_real]
            ka_s.append(k_new.reshape(db, 1, KV_A, LANE))
            va_s.append(v_new.reshape(db, 1, KV_A, LANE))
            lam_full, lam_init = _diff_lambda(lam_a[j], i)
            attn_p = _diffattn_prompt(q_hm, kv_hm, lam_full, lam_init, subln_a[j], batch, t_len)
            attn_s = _diffattn_sample(_sample_rows(q_hm, n_p, db), k_new, v_new, cache_k, cache_v, j,
                                      page_table, lam_full, lam_init, subln_a[j]).reshape(db, H_A * LANE)
            w_o, lay_o, b_o = w_o_a, j, zero_bias
        elif i % 3 == 1:
            nq = H_B * HD_B
            w_q = w_qkv_b[j][:, :nq][:, perm][None]
            b_q = b_qkv_b[j][:nq][perm]
            w_kv = w_qkv_b[j][:, nq:][None]
            b_kv = b_qkv_b[j][nq:]
            (q_hm,) = _proj(yb, w_q, b_q, layer=0, col0=0, ncols=nq, cw=512, hm_dtype=BF16, emit_tok=None)
            kv_hm, kv_tok = _proj(yb, w_kv, b_kv, layer=0, col0=0, ncols=2 * LANE, cw=2 * LANE,
                                  hm_dtype=BF16, emit_tok="2d")
            k_tok = kv_tok[:, :LANE]
            v_tok = kv_tok[:, LANE:]
            last_w = lambda a: a[:n_p].reshape(batch, t_len, LANE)[:, t_len - WINDOW:].reshape(
                batch, WINDOW, KV_B, HD_B)
            kb_p.append(last_w(k_tok))
            vb_p.append(last_w(v_tok))
            k_new = k_tok[n_p:n_real].reshape(db, 1, LANE)
            v_new = v_tok[n_p:n_real].reshape(db, 1, LANE)
            k_buf = state_swa_k[j].reshape(db, WINDOW, LANE)
            v_buf = state_swa_v[j].reshape(db, WINDOW, LANE)
            kb_s.append(jnp.concatenate([k_buf[:, 1:], k_new], axis=1).reshape(db, WINDOW, KV_B, HD_B))
            vb_s.append(jnp.concatenate([v_buf[:, 1:], v_new], axis=1).reshape(db, WINDOW, KV_B, HD_B))
            attn_p = _swa_prompt(q_hm, kv_hm, sinks_b[j], batch, t_len)
            attn_s = _swa_sample(_sample_rows(q_hm, n_p, db), k_buf, v_buf, k_new, v_new,
                                 sinks_b[j]).reshape(db, G_B * LANE)
            w_o, lay_o, b_o = w_o_b[j][perm][None], 0, b_o_b[j]
        else:
            n_in = w_in_c.shape[2]
            n_main = 2 * H_C * DK_C + 2 * H_C * DV_C
            w_in = jnp.zeros((1, D_MODEL, n_main + LANE), F32).at[0, :, :n_in].set(w_in_c[j])
            (hm,) = _proj(yb, w_in, jnp.zeros((n_main + LANE,), F32), layer=0, col0=0,
                          ncols=n_main + LANE, cw=5 * LANE, hm_dtype=F32, emit_tok=None)
            wg2p = jnp.zeros((LANE, H_C * DK_C), F32).at[:GATE_RANK].set(w_gate2_c[j])
            attn_p, st_p = _gla_prompt(hm, wg2p, b_gate_c[j], b_r_c[j], gn_c[j], batch, t_len)
            gc_p.append(st_p)
            hs = _sample_rows(hm, n_p, db)
            q_s = hs[:, 0:H_C]
            k_s = hs[:, H_C:2 * H_C]
            v_s = hs[:, 2 * H_C:4 * H_C].reshape(db, H_C, DV_C)
            r_s = hs[:, 4 * H_C:6 * H_C].reshape(db, H_C, DV_C)
            gt_s = hs[:, 6 * H_C:6 * H_C + 1]
            attn_s, st_s = _gla_sample(q_s, k_s, v_s, r_s, gt_s, state_gla[j].astype(F32), wg2p,
                                       b_gate_c[j], b_r_c[j], gn_c[j])
            gc_s.append(st_s)
            attn_s = attn_s.reshape(db, H_C * DV_C)
            w_o, lay_o, b_o = w_o_c, j, zero_bias

        f = i // 2
        if i % 2 == 0:
            y1, y1b = _outproj_ln(attn_p, attn_s, w_o, lay_o, b_o, y, ln1_g[i], ln1_b[i])
            y, yb = _ffn(y1b, w_gu_d, w_down_d, jnp.full((n_ffn_tiles,), f, jnp.int32),
                         jnp.ones((n_ffn_tiles,), jnp.int32), tm=ffn_tile,
                         dense_args=(y1, ln2_g[i], ln2_b[i]))
        else:
            y1, _, route = _outproj_ln(attn_p, attn_s, w_o, lay_o, b_o, y, ln1_g[i], ln1_b[i], w_router[f])
            y, yb = _moe(y1, route, n_real, w_gu_e2, w_down_e2, f * N_EXPERTS, ln2_g[i], ln2_b[i])

    yp = y[:n_p].reshape(batch, t_len, D_MODEL)
    ys = y[n_p:n_real].reshape(db, 1, D_MODEL)
    return (yp, ys, jnp.stack(ka_p), jnp.stack(va_p), jnp.stack(ka_s), jnp.stack(va_s),
            jnp.stack(kb_p), jnp.stack(vb_p), jnp.stack(kb_s), jnp.stack(vb_s),
            jnp.stack(gc_p), jnp.stack(gc_s))
```
